```python
import math
import jax, jax.numpy as jnp
from jax import lax
import numpy as np

D_MODEL = 2048
BATCH = 4
SEQ = 2048
DEPTH = 4

DA_HEADS = 8
DA_HEAD_DIM = 64
DA_V_DIM = 2 * DA_HEAD_DIM
MLA_HEADS = 8
MLA_Q_RANK = 768
MLA_KV_RANK = 512
MLA_NOPE_DIM = 128
MLA_ROPE_DIM = 64
MLA_V_DIM = 128
ROPE_THETA = 10000.0
REL_BUCKETS = 32
REL_MAX_DIST = 128
N_GROUPS = 4
EXPERTS_PER_GROUP = 8
N_EXPERTS = N_GROUPS * EXPERTS_PER_GROUP
TOP_K = 2
D_EXPERT = 704
MOE_BLOCK = 128
Q_BLOCK = 128
NORM_EPS = 1e-5
NEG_INF = -1e30
DEEPNORM_ALPHA = (2 * DEPTH) ** 0.25
DEEPNORM_BETA = (8 * DEPTH) ** -0.25

DA_QK_W = DA_HEADS * 2 * DA_HEAD_DIM
DA_V_W = DA_HEADS * DA_V_DIM
MLA_O_W = MLA_HEADS * MLA_V_DIM
IN_WIDTHS = (DA_QK_W, DA_QK_W, DA_V_W, MLA_Q_RANK, MLA_KV_RANK, MLA_ROPE_DIM, D_MODEL, D_MODEL)
IN_WIDTH = sum(IN_WIDTHS)
IN_SPLITS = tuple(sum(IN_WIDTHS[:i + 1]) for i in range(len(IN_WIDTHS) - 1))

kernel_name = 'hybrid_diffattn_mla_hmoe_deepnorm'


def layer_norm(x, g, b):
    xf = x.astype(jnp.float32)
    mu = jnp.mean(xf, axis=-1, keepdims=True)
    var = jnp.mean(jnp.square(xf - mu), axis=-1, keepdims=True)
    return ((xf - mu) * lax.rsqrt(var + NORM_EPS) * g + b).astype(x.dtype)


def rms_norm(x, g):
    xf = x.astype(jnp.float32)
    return (xf * lax.rsqrt(jnp.mean(xf * xf, axis=-1, keepdims=True) + NORM_EPS) * g).astype(x.dtype)


def rope_tables(seq):
    inv = ROPE_THETA ** (-jnp.arange(0, MLA_ROPE_DIM, 2, dtype=jnp.float32) / MLA_ROPE_DIM)
    ang = jnp.arange(seq, dtype=jnp.float32)[:, None] * inv[None, :]
    return jnp.cos(ang), jnp.sin(ang)


def rotary(x, cos, sin):
    shape = (1, cos.shape[0]) + (1,) * (x.ndim - 3) + (cos.shape[1],)
    c = cos.reshape(shape)
    s = sin.reshape(shape)
    x1, x2 = jnp.split(x.astype(jnp.float32), 2, axis=-1)
    return jnp.concatenate([x1 * c - x2 * s, x2 * c + x1 * s], axis=-1).astype(x.dtype)


def t5_bucket(q_pos, k_pos):
    n = jnp.maximum(q_pos[:, None] - k_pos[None, :], 0)
    max_exact = REL_BUCKETS // 2
    nf = jnp.maximum(n, 1).astype(jnp.float32)
    large = max_exact + (jnp.log(nf / max_exact) / math.log(REL_MAX_DIST / max_exact)
                         * (REL_BUCKETS - max_exact)).astype(jnp.int32)
    large = jnp.minimum(large, REL_BUCKETS - 1)
    return jnp.where(n < max_exact, n, large)


def diff_attention(q, k, v, lam, lam_init, subln_g, rel_bias):
    B, S = q.shape[0], q.shape[1]
    scale = DA_HEAD_DIM ** -0.5
    outs = []
    for start in range(0, S, Q_BLOCK):
        end = start + Q_BLOCK
        q_pos = jnp.arange(start, end)
        k_pos = jnp.arange(end)
        s = jnp.einsum('bqhmd,bkhmd->bhmqk', q[:, start:end], k[:, :end]).astype(jnp.float32) * scale
        bias = jnp.transpose(rel_bias[t5_bucket(q_pos, k_pos)], (2, 0, 1)).astype(jnp.float32)
        s = s + bias[None, :, None]
        s = jnp.where(k_pos[None, :] <= q_pos[:, None], s, NEG_INF)
        p = jax.nn.softmax(s, axis=-1)
        a = p[:, :, 0] - lam * p[:, :, 1]
        outs.append(jnp.einsum('bhqk,bkhv->bqhv', a.astype(v.dtype), v[:, :end]))
    o = jnp.concatenate(outs, axis=1)
    o = rms_norm(o, subln_g) * (1.0 - lam_init)
    return o.reshape(B, S, DA_V_W)


def mla_attention(c_q, c_kv, k_rope, q_norm_g, w_uq, kv_norm_g, w_ukv, cos, sin):
    B, S = c_q.shape[0], c_q.shape[1]
    q = (rms_norm(c_q, q_norm_g) @ w_uq).reshape(B, S, MLA_HEADS, MLA_NOPE_DIM + MLA_ROPE_DIM)
    q_nope = q[..., :MLA_NOPE_DIM]
    q_rope = rotary(q[..., MLA_NOPE_DIM:], cos, sin)
    kv = (rms_norm(c_kv, kv_norm_g) @ w_ukv).reshape(B, S, MLA_HEADS, MLA_NOPE_DIM + MLA_V_DIM)
    k_nope = kv[..., :MLA_NOPE_DIM]
    v = kv[..., MLA_NOPE_DIM:]
    k_rope = rotary(k_rope, cos, sin)
    scale = (MLA_NOPE_DIM + MLA_ROPE_DIM) ** -0.5
    outs = []
    for start in range(0, S, Q_BLOCK):
        end = start + Q_BLOCK
        q_pos = jnp.arange(start, end)
        k_pos = jnp.arange(end)
        s = (jnp.einsum('bqhd,bkhd->bhqk', q_nope[:, start:end], k_nope[:, :end])
             + jnp.einsum('bqhr,bkr->bhqk', q_rope[:, start:end], k_rope[:, :end])).astype(jnp.float32) * scale
        s = jnp.where(k_pos[None, :] <= q_pos[:, None], s, NEG_INF)
        p = jax.nn.softmax(s, axis=-1)
        outs.append(jnp.einsum('bhqk,bkhv->bqhv', p.astype(v.dtype), v[:, :end]))
    return jnp.concatenate(outs, axis=1).reshape(B, S, MLA_O_W)


def hybrid_mixer(x, w_in, da_lambda, da_subln_g, lam_init, rel_bias, mla_q_norm_g, mla_w_uq,
                 mla_kv_norm_g, mla_w_ukv, w_branch_a, w_branch_b, w_out, cos, sin):
    B, S, _ = x.shape
    h = x @ w_in
    q_da, k_da, v_da, c_q, c_kv, k_rope, gate_a, gate_b = jnp.split(h, IN_SPLITS, axis=-1)
    q_da = q_da.reshape(B, S, DA_HEADS, 2, DA_HEAD_DIM)
    k_da = k_da.reshape(B, S, DA_HEADS, 2, DA_HEAD_DIM)
    v_da = v_da.reshape(B, S, DA_HEADS, DA_V_DIM)
    lq1, lk1, lq2, lk2 = da_lambda.astype(jnp.float32)
    lam = jnp.exp(jnp.sum(lq1 * lk1)) - jnp.exp(jnp.sum(lq2 * lk2)) + lam_init
    o_a = diff_attention(q_da, k_da, v_da, lam, lam_init, da_subln_g, rel_bias)
    o_b = mla_attention(c_q, c_kv, k_rope, mla_q_norm_g, mla_w_uq, mla_kv_norm_g, mla_w_ukv, cos, sin)
    y = jax.nn.sigmoid(gate_a) * (o_a @ w_branch_a) + jax.nn.sigmoid(gate_b) * (o_b @ w_branch_b)
    return y @ w_out


def hier_moe(x, w_rg, b_rg, w_re, b_re, w_gate, w_up, w_down):
    B, S, D = x.shape
    T = B * S
    xt = x.reshape(T, D)
    xf = xt.astype(jnp.float32)
    g_logits = xf @ w_rg.astype(jnp.float32) + b_rg.astype(jnp.float32)
    g_prob = jax.nn.softmax(g_logits, axis=-1)
    g_idx = jnp.argmax(g_logits, axis=-1)
    g_w = jnp.take_along_axis(g_prob, g_idx[:, None], axis=1)[:, 0]
    e_logits = (xf @ w_re.astype(jnp.float32) + b_re.astype(jnp.float32)).reshape(T, N_GROUPS, EXPERTS_PER_GROUP)
    e_sel = jnp.take_along_axis(e_logits, g_idx[:, None, None], axis=1)[:, 0]
    top_p, top_i = lax.top_k(jax.nn.softmax(e_sel, axis=-1), TOP_K)
    top_p = top_p / jnp.sum(top_p, axis=-1, keepdims=True)
    expert_id = g_idx[:, None] * EXPERTS_PER_GROUP + top_i
    weight = g_w[:, None] * top_p
    A = T * TOP_K
    flat_e = expert_id.reshape(A).astype(jnp.int32)
    flat_t = jnp.repeat(jnp.arange(T, dtype=jnp.int32), TOP_K)
    flat_w = weight.reshape(A)
    order = jnp.argsort(flat_e)
    se, st, sw = flat_e[order], flat_t[order], flat_w[order]
    counts = jnp.bincount(flat_e, length=N_EXPERTS)
    padded = (counts + MOE_BLOCK - 1) // MOE_BLOCK * MOE_BLOCK
    start = jnp.cumsum(counts) - counts
    pend = jnp.cumsum(padded)
    pstart = pend - padded
    dest = pstart[se] + jnp.arange(A, dtype=jnp.int32) - start[se]
    n_blocks = -(-A // MOE_BLOCK) + N_EXPERTS
    P = n_blocks * MOE_BLOCK
    row_tok = jnp.full((P,), T, jnp.int32).at[dest].set(st)
    row_w = jnp.zeros((P,), xt.dtype).at[dest].set(sw.astype(xt.dtype))
    block_e = jnp.clip(jnp.searchsorted(pend, jnp.arange(n_blocks) * MOE_BLOCK, side='right'), 0, N_EXPERTS - 1)
    x_pad = jnp.concatenate([xt, jnp.zeros((1, D), xt.dtype)], axis=0)
    xb = x_pad[row_tok].reshape(n_blocks, MOE_BLOCK, D)

    def expert_block(args):
        xblk, e = args
        hid = jax.nn.silu(xblk @ w_gate[e]) * (xblk @ w_up[e])
        return hid @ w_down[e]

    yb = lax.map(expert_block, (xb, block_e))
    y = jnp.zeros((T + 1, D), xt.dtype).at[row_tok].add(yb.reshape(P, D) * row_w[:, None])
    return y[:T].reshape(B, S, D)


def setup_inputs(seed: int = 0) -> dict:
    key = jax.random.key(seed)
    ks = jax.random.split(key, 24)
    L = DEPTH

    def nrm(k, shape, scale):
        return jax.random.normal(k, shape, jnp.float32) * scale

    return {
        'x': nrm(ks[0], (BATCH, SEQ, D_MODEL), 1.0),
        'w_in': nrm(ks[1], (L, D_MODEL, IN_WIDTH), D_MODEL ** -0.5),
        'da_lambda': nrm(ks[2], (L, 4, DA_HEAD_DIM), 0.1),
        'da_subln_g': 1.0 + nrm(ks[3], (L, DA_V_DIM), 0.02),
        'mla_q_norm_g': 1.0 + nrm(ks[4], (L, MLA_Q_RANK), 0.02),
        'mla_w_uq': nrm(ks[5], (L, MLA_Q_RANK, MLA_HEADS * (MLA_NOPE_DIM + MLA_ROPE_DIM)), MLA_Q_RANK ** -0.5),
        'mla_kv_norm_g': 1.0 + nrm(ks[6], (L, MLA_KV_RANK), 0.02),
        'mla_w_ukv': nrm(ks[7], (L, MLA_KV_RANK, MLA_HEADS * (MLA_NOPE_DIM + MLA_V_DIM)), MLA_KV_RANK ** -0.5),
        'w_branch_a': nrm(ks[8], (L, DA_V_W, D_MODEL), DA_V_W ** -0.5),
        'w_branch_b': nrm(ks[9], (L, MLA_O_W, D_MODEL), MLA_O_W ** -0.5),
        'w_out': nrm(ks[10], (L, D_MODEL, D_MODEL), DEEPNORM_BETA * D_MODEL ** -0.5),
        'rel_bias': nrm(ks[11], (REL_BUCKETS, DA_HEADS), 0.3),
        'ln1_g': 1.0 + nrm(ks[12], (L, D_MODEL), 0.02),
        'ln1_b': nrm(ks[13], (L, D_MODEL), 0.02),
        'router_w_group': nrm(ks[14], (L, D_MODEL, N_GROUPS), D_MODEL ** -0.5),
        'router_b_group': nrm(ks[15], (L, N_GROUPS), 0.01),
        'router_w_expert': nrm(ks[16], (L, D_MODEL, N_EXPERTS), D_MODEL ** -0.5),
        'router_b_expert': nrm(ks[17], (L, N_EXPERTS), 0.01),
        'expert_w_gate': nrm(ks[18], (L, N_EXPERTS, D_MODEL, D_EXPERT), D_MODEL ** -0.5),
        'expert_w_up': nrm(ks[19], (L, N_EXPERTS, D_MODEL, D_EXPERT), D_MODEL ** -0.5),
        'expert_w_down': nrm(ks[20], (L, N_EXPERTS, D_EXPERT, D_MODEL), DEEPNORM_BETA * D_EXPERT ** -0.5),
        'ln2_g': 1.0 + nrm(ks[21], (L, D_MODEL), 0.02),
        'ln2_b': nrm(ks[22], (L, D_MODEL), 0.02),
    }


def reference(x, w_in, da_lambda, da_subln_g, mla_q_norm_g, mla_w_uq, mla_kv_norm_g, mla_w_ukv,
              w_branch_a, w_branch_b, w_out, rel_bias, ln1_g, ln1_b, router_w_group, router_b_group,
              router_w_expert, router_b_expert, expert_w_gate, expert_w_up, expert_w_down, ln2_g, ln2_b):
    cos, sin = rope_tables(x.shape[1])
    for l in range(DEPTH):
        lam_init = 0.8 - 0.6 * math.exp(-0.3 * l)
        mix = hybrid_mixer(x, w_in[l], da_lambda[l], da_subln_g[l], lam_init, rel_bias,
                           mla_q_norm_g[l], mla_w_uq[l], mla_kv_norm_g[l], mla_w_ukv[l],
                           w_branch_a[l], w_branch_b[l], w_out[l], cos, sin)
        x = layer_norm(DEEPNORM_ALPHA * x + mix, ln1_g[l], ln1_b[l])
        ffn = hier_moe(x, router_w_group[l], router_b_group[l], router_w_expert[l], router_b_expert[l],
                       expert_w_gate[l], expert_w_up[l], expert_w_down[l])
        x = layer_norm(DEEPNORM_ALPHA * x + ffn, ln2_g[l], ln2_b[l])
    return x
```

```python
import functools
import math

import jax
import jax.numpy as jnp
from jax import lax
from jax.experimental import pallas as pl
from jax.experimental.pallas import tpu as pltpu

F32 = jnp.float32
BF16 = jnp.bfloat16
I32 = jnp.int32
SDS = jax.ShapeDtypeStruct

DA_HEADS = 8
DA_HEAD_DIM = 64
DA_V_DIM = 2 * DA_HEAD_DIM
MLA_HEADS = 8
MLA_Q_RANK = 768
MLA_KV_RANK = 512
MLA_NOPE_DIM = 128
MLA_ROPE_DIM = 64
MLA_V_DIM = 128
ROPE_THETA = 10000.0
REL_BUCKETS = 32
REL_MAX_DIST = 128
REL_MAX_EXACT = REL_BUCKETS // 2
N_GROUPS = 4
EXPERTS_PER_GROUP = 8
N_EXPERTS = N_GROUPS * EXPERTS_PER_GROUP
TOP_K = 2
NORM_EPS = 1e-5
NEG_INF = -1e30

DA_QK_W = DA_HEADS * 2 * DA_HEAD_DIM
DA_V_W = DA_HEADS * DA_V_DIM
MLA_O_W = MLA_HEADS * MLA_V_DIM
MLA_QCAT = 2 * MLA_NOPE_DIM

LANES = 128
VMEM_LIMIT = 56 * 1024 * 1024

ATT_BLOCK = 256
MOE_BLOCK = 256
ROUTE_W = LANES


def _cparams(n_axes):
    return pltpu.CompilerParams(dimension_semantics=("arbitrary",) * n_axes,
                                vmem_limit_bytes=VMEM_LIMIT)


def _mm_kernel(a_ref, w_ref, o_ref):
    o_ref[...] = jnp.dot(a_ref[...], w_ref[...].astype(BF16),
                         preferred_element_type=F32).astype(o_ref.dtype)


def _matmul(a, w3, l, col0, n_cols, tm, tn, out_dtype, name):
    m, k = a.shape
    tm = min(tm, m)
    assert m % tm == 0 and col0 % tn == 0 and n_cols % tn == 0
    c0 = col0 // tn
    return pl.pallas_call(
        _mm_kernel,
        out_shape=SDS((m, n_cols), out_dtype),
        grid=(m // tm, n_cols // tn),
        in_specs=[pl.BlockSpec((tm, k), lambda i, j: (i, 0)),
                  pl.BlockSpec((None, k, tn), lambda i, j: (l, 0, c0 + j))],
        out_specs=pl.BlockSpec((tm, tn), lambda i, j: (i, j)),
        compiler_params=_cparams(2),
        name=name,
    )(a, w3)


def _krope_kernel(a_ref, w_ref, tab_ref, o_ref):
    t = jnp.dot(a_ref[...], w_ref[...].astype(BF16), preferred_element_type=F32)
    t = t * tab_ref[...]
    o_ref[...] = (t + pltpu.roll(t, MLA_ROPE_DIM, axis=1)).astype(o_ref.dtype)


def _krope(xb, w_kr, tabk, seq, tm):
    m, k = xb.shape
    tm = min(tm, seq)
    nsb = seq // tm
    return pl.pallas_call(
        _krope_kernel,
        out_shape=SDS((m, LANES), BF16),
        grid=(m // tm,),
        in_specs=[pl.BlockSpec((tm, k), lambda i: (i, 0)),
                  pl.BlockSpec((k, LANES), lambda i: (0, 0)),
                  pl.BlockSpec((tm, LANES), lambda i: (i % nsb, 0))],
        out_specs=pl.BlockSpec((tm, LANES), lambda i: (i, 0)),
        compiler_params=_cparams(1),
        name="krope",
    )(xb, w_kr, tabk)


def _rms(c, g):
    return c * lax.rsqrt(jnp.mean(c * c, axis=-1, keepdims=True) + NORM_EPS) * g


def _uq_kernel(c_ref, g_ref, w_ref, tab_ref, o_ref):
    n = _rms(c_ref[...].astype(F32), g_ref[...])
    acc = jnp.dot(n.astype(BF16), w_ref[...].astype(BF16), preferred_element_type=F32)
    tab = tab_ref[...]
    for h in range(MLA_HEADS):
        sl = slice(h * MLA_QCAT, (h + 1) * MLA_QCAT)
        o_ref[:, sl] = (acc[:, sl] * tab).astype(o_ref.dtype)


def _uq(h_main, g, w_uq2, tabq, seq, tm):
    m = h_main.shape[0]
    tm = min(tm, seq)
    nsb = seq // tm
    cq_blk = (DA_QK_W * 2 + DA_V_W) // MLA_Q_RANK
    n_out = MLA_HEADS * MLA_QCAT
    return pl.pallas_call(
        _uq_kernel,
        out_shape=SDS((m, n_out), BF16),
        grid=(m // tm,),
        in_specs=[pl.BlockSpec((tm, MLA_Q_RANK), lambda i: (i, cq_blk)),
                  pl.BlockSpec((1, MLA_Q_RANK), lambda i: (0, 0)),
                  pl.BlockSpec((MLA_Q_RANK, n_out), lambda i: (0, 0)),
                  pl.BlockSpec((tm, MLA_QCAT), lambda i: (i % nsb, 0))],
        out_specs=pl.BlockSpec((tm, n_out), lambda i: (i, 0)),
        compiler_params=_cparams(1),
        name="mla_uq",
    )(h_main, g, w_uq2, tabq)


def _ukv_kernel(c0_ref, c1_ref, g_ref, w_ref, o_ref):
    c = jnp.concatenate([c0_ref[...], c1_ref[...]], axis=1).astype(F32)
    n = _rms(c, g_ref[...])
    o_ref[...] = jnp.dot(n.astype(BF16), w_ref[...].astype(BF16),
                         preferred_element_type=F32).astype(o_ref.dtype)


def _ukv(h_main, g, w_ukv2, tm):
    m = h_main.shape[0]
    tm = min(tm, m)
    half = MLA_KV_RANK // 2
    b0 = (DA_QK_W * 2 + DA_V_W + MLA_Q_RANK) // half
    n_out = w_ukv2.shape[1]
    return pl.pallas_call(
        _ukv_kernel,
        out_shape=SDS((m, n_out), BF16),
        grid=(m // tm,),
        in_specs=[pl.BlockSpec((tm, half), lambda i: (i, b0)),
                  pl.BlockSpec((tm, half), lambda i: (i, b0 + 1)),
                  pl.BlockSpec((1, MLA_KV_RANK), lambda i: (0, 0)),
                  pl.BlockSpec((MLA_KV_RANK, n_out), lambda i: (0, 0))],
        out_specs=pl.BlockSpec((tm, n_out), lambda i: (i, 0)),
        compiler_params=_cparams(1),
        name="mla_ukv",
    )(h_main, h_main, g, w_ukv2)


def _flash_init(m_scr, l_scr, acc_scr):
    m_scr[...] = jnp.full(m_scr.shape, NEG_INF, F32)
    l_scr[...] = jnp.zeros(l_scr.shape, F32)
    acc_scr[...] = jnp.zeros(acc_scr.shape, F32)


def _flash_update(s, v, m_scr, l_scr, acc_scr):
    m_prev = m_scr[...]
    m_new = jnp.maximum(m_prev, jnp.max(s, axis=1, keepdims=True))
    p = jnp.exp(s - m_new)
    alpha = jnp.exp(m_prev - m_new)
    l_scr[...] = alpha * l_scr[...] + jnp.sum(p, axis=1, keepdims=True)
    acc_scr[...] = alpha * acc_scr[...] + jnp.dot(p.astype(BF16), v, preferred_element_type=F32)
    m_scr[...] = m_new


def _dot_nt(a, b):
    return lax.dot_general(a, b, (((1,), (1,)), ((), ())), preferred_element_type=F32)


def _t5_bias_blocks(rb_ref, h, bias_scr, blk):
    row = lax.broadcasted_iota(I32, (blk, blk), 0)
    col = lax.broadcasted_iota(I32, (blk, blk), 1)
    far = rb_ref[REL_BUCKETS - 1, h]
    for d in range(2):
        n = row - col + d * blk
        nn = jnp.maximum(n, 0)
        nf = jnp.maximum(nn, 1).astype(F32)
        large = REL_MAX_EXACT + (jnp.log(nf / REL_MAX_EXACT) / math.log(REL_MAX_DIST / REL_MAX_EXACT)
                                 * (REL_BUCKETS - REL_MAX_EXACT)).astype(I32)
        large = jnp.minimum(large, REL_BUCKETS - 1)
        bucket = jnp.where(nn < REL_MAX_EXACT, nn, large)
        val = jnp.zeros((blk, blk), F32)
        for bb in range(REL_BUCKETS):
            val = jnp.where(bucket == bb, rb_ref[bb, h], val)
        bias_scr[d] = jnp.where(n >= 0, val - far, NEG_INF)


def _da_kernel(rb_ref, lamv_ref, q_ref, k_ref, v_ref, g_ref, o_ref,
               bias_scr, m_scr, l_scr, acc_scr, *, blk, lam_init):
    h = pl.program_id(0)
    b = pl.program_id(1)
    qi = pl.program_id(2)

    @pl.when((b == 0) & (qi == 0))
    def _():
        _t5_bias_blocks(rb_ref, h, bias_scr, blk)

    q = q_ref[...]
    lane = lax.broadcasted_iota(I32, q.shape, 1)
    scale = DA_HEAD_DIM ** -0.5
    zero = jnp.zeros_like(q)
    q2 = jnp.concatenate([jnp.where(lane < DA_HEAD_DIM, q, zero),
                          jnp.where(lane >= DA_HEAD_DIM, q, zero)], axis=0) * scale

    _flash_init(m_scr, l_scr, acc_scr)

    def block(j, bias):
        off = pl.multiple_of(j * blk, blk)
        s = _dot_nt(q2, k_ref[pl.ds(off, blk), :])
        if bias is not None:
            s = s + jnp.concatenate([bias, bias], axis=0)
        _flash_update(s, v_ref[pl.ds(off, blk), :], m_scr, l_scr, acc_scr)

    def far_body(j, c):
        block(j, None)
        return c

    lax.fori_loop(0, qi - 1, far_body, 0)

    @pl.when(qi >= 1)
    def _():
        block(qi - 1, bias_scr[1])

    block(qi, bias_scr[0])

    lamv = lamv_ref[...]
    lam = (jnp.exp(jnp.sum(lamv[0:1] * lamv[1:2], axis=1, keepdims=True))
           - jnp.exp(jnp.sum(lamv[2:3] * lamv[3:4], axis=1, keepdims=True)) + lam_init)
    o12 = acc_scr[...] / l_scr[...]
    o = o12[:blk] - lam * o12[blk:]
    o = o * lax.rsqrt(jnp.mean(o * o, axis=1, keepdims=True) + NORM_EPS) * g_ref[...] * (1.0 - lam_init)
    o_ref[...] = o.astype(o_ref.dtype)


def _da_attention(h_main, rel_bias, da_lambda, subln_g, l, lam_init, batch, seq):
    blk = min(ATT_BLOCK, seq)
    assert seq % blk == 0 and blk + 1 >= REL_MAX_DIST
    nq = seq // blk
    hd = 2 * DA_HEAD_DIM
    kcol = DA_QK_W // hd
    vcol = 2 * DA_QK_W // DA_V_DIM
    return pl.pallas_call(
        functools.partial(_da_kernel, blk=blk, lam_init=lam_init),
        out_shape=SDS((batch * seq, DA_V_W), BF16),
        grid=(DA_HEADS, batch, nq),
        in_specs=[pl.BlockSpec(memory_space=pltpu.SMEM),
                  pl.BlockSpec((None, 4, DA_HEAD_DIM), lambda h, b, qi: (l, 0, 0)),
                  pl.BlockSpec((blk, hd), lambda h, b, qi: (b * nq + qi, h)),
                  pl.BlockSpec((seq, hd), lambda h, b, qi: (b, kcol + h)),
                  pl.BlockSpec((seq, DA_V_DIM), lambda h, b, qi: (b, vcol + h)),
                  pl.BlockSpec((None, 1, DA_V_DIM), lambda h, b, qi: (l, 0, 0))],
        out_specs=pl.BlockSpec((blk, DA_V_DIM), lambda h, b, qi: (b * nq + qi, h)),
        scratch_shapes=[pltpu.VMEM((2, blk, blk), F32),
                        pltpu.VMEM((2 * blk, 1), F32),
                        pltpu.VMEM((2 * blk, 1), F32),
                        pltpu.VMEM((2 * blk, DA_V_DIM), F32)],
        compiler_params=_cparams(3),
        name="da_attn",
    )(rel_bias, da_lambda, h_main, h_main, h_main, subln_g)


def _mla_kernel(q_ref, kn_ref, kr_ref, v_ref, o_ref, m_scr, l_scr, acc_scr, *, blk):
    qi = pl.program_id(2)
    q = q_ref[...]
    _flash_init(m_scr, l_scr, acc_scr)

    def block(j, masked):
        off = pl.multiple_of(j * blk, blk)
        kcat = jnp.concatenate([kn_ref[pl.ds(off, blk), :], kr_ref[pl.ds(off, blk), :]], axis=1)
        s = _dot_nt(q, kcat)
        if masked:
            row = lax.broadcasted_iota(I32, s.shape, 0)
            col = lax.broadcasted_iota(I32, s.shape, 1)
            s = jnp.where(col <= row, s, NEG_INF)
        _flash_update(s, v_ref[pl.ds(off, blk), :], m_scr, l_scr, acc_scr)

    def far_body(j, c):
        block(j, False)
        return c

    lax.fori_loop(0, qi, far_body, 0)
    block(qi, True)
    o_ref[...] = (acc_scr[...] / l_scr[...]).astype(o_ref.dtype)


def _mla_attention(q_cat, kv, kr2, batch, seq):
    blk = min(ATT_BLOCK, seq)
    nq = seq // blk
    return pl.pallas_call(
        functools.partial(_mla_kernel, blk=blk),
        out_shape=SDS((batch * seq, MLA_O_W), BF16),
        grid=(batch, MLA_HEADS, nq),
        in_specs=[pl.BlockSpec((blk, MLA_QCAT), lambda b, h, qi: (b * nq + qi, h)),
                  pl.BlockSpec((seq, MLA_NOPE_DIM), lambda b, h, qi: (b, h)),
                  pl.BlockSpec((seq, LANES), lambda b, h, qi: (b, 0)),
                  pl.BlockSpec((seq, MLA_V_DIM), lambda b, h, qi: (b, MLA_HEADS + h))],
        out_specs=pl.BlockSpec((blk, MLA_V_DIM), lambda b, h, qi: (b * nq + qi, h)),
        scratch_shapes=[pltpu.VMEM((blk, 1), F32),
                        pltpu.VMEM((blk, 1), F32),
                        pltpu.VMEM((blk, MLA_V_DIM), F32)],
        compiler_params=_cparams(3),
        name="mla_attn",
    )(q_cat, kv, kr2, kv)


def _sigmoid(x):
    return 1.0 / (1.0 + jnp.exp(-x))


def _gated_kernel(oa_ref, ob_ref, wa_ref, wb_ref, ga_ref, gb_ref, o_ref):
    ya = jnp.dot(oa_ref[...], wa_ref[...].astype(BF16), preferred_element_type=F32)
    yb = jnp.dot(ob_ref[...], wb_ref[...].astype(BF16), preferred_element_type=F32)
    y = _sigmoid(ga_ref[...].astype(F32)) * ya + _sigmoid(gb_ref[...].astype(F32)) * yb
    o_ref[...] = y.astype(o_ref.dtype)


def _gated(o_a, o_b, w_a, w_b, h_g, l, tm, tn):
    m = o_a.shape[0]
    d = w_a.shape[2]
    tm = min(tm, m)
    ngb = d // tn
    return pl.pallas_call(
        _gated_kernel,
        out_shape=SDS((m, d), BF16),
        grid=(m // tm, ngb),
        in_specs=[pl.BlockSpec((tm, DA_V_W), lambda i, j: (i, 0)),
                  pl.BlockSpec((tm, MLA_O_W), lambda i, j: (i, 0)),
                  pl.BlockSpec((None, DA_V_W, tn), lambda i, j: (l, 0, j)),
                  pl.BlockSpec((None, MLA_O_W, tn), lambda i, j: (l, 0, j)),
                  pl.BlockSpec((tm, tn), lambda i, j: (i, j)),
                  pl.BlockSpec((tm, tn), lambda i, j: (i, ngb + j))],
        out_specs=pl.BlockSpec((tm, tn), lambda i, j: (i, j)),
        compiler_params=_cparams(2),
        name="gated",
    )(o_a, o_b, w_a, w_b, h_g, h_g)


def _layer_norm(z, g, b):
    mu = jnp.mean(z, axis=-1, keepdims=True)
    zc = z - mu
    var = jnp.mean(zc * zc, axis=-1, keepdims=True)
    return zc * lax.rsqrt(var + NORM_EPS) * g + b


def _split_bf16(x):
    hi = x.astype(BF16)
    lo = (x - hi.astype(F32)).astype(BF16)
    return hi, lo


def _route_record(lg):
    lane = lax.broadcasted_iota(I32, lg.shape, 1)
    lane_f = lane.astype(F32)
    big = float(ROUTE_W)
    gmask = lane < N_GROUPS
    g_max = jnp.max(jnp.where(gmask, lg, NEG_INF), axis=1, keepdims=True)
    g_idx = jnp.min(jnp.where(gmask & (lg == g_max), lane_f, big), axis=1, keepdims=True)
    g_w = 1.0 / jnp.sum(jnp.where(gmask, jnp.exp(lg - g_max), 0.0), axis=1, keepdims=True)

    lo = N_GROUPS + EXPERTS_PER_GROUP * g_idx
    sel = (lane_f >= lo) & (lane_f < lo + EXPERTS_PER_GROUP)
    e_max = jnp.max(jnp.where(sel, lg, NEG_INF), axis=1, keepdims=True)
    pe = jnp.where(sel, jnp.exp(lg - e_max), 0.0)
    prob = pe / jnp.sum(pe, axis=1, keepdims=True)
    p1 = jnp.max(jnp.where(sel, prob, -1.0), axis=1, keepdims=True)
    i1 = jnp.min(jnp.where(sel & (prob == p1), lane_f, big), axis=1, keepdims=True)
    sel2 = sel & (lane_f != i1)
    p2 = jnp.max(jnp.where(sel2, prob, -1.0), axis=1, keepdims=True)
    i2 = jnp.min(jnp.where(sel2 & (prob == p2), lane_f, big), axis=1, keepdims=True)
    tot = p1 + p2
    w1 = g_w * (p1 / tot)
    w2 = g_w * (p2 / tot)
    rec = jnp.where(lane == 0, i1 - N_GROUPS,
          jnp.where(lane == 1, i2 - N_GROUPS,
          jnp.where(lane == 2, w1, jnp.where(lane == 3, w2, 0.0))))
    return rec


def _outproj_kernel(y_ref, w_ref, x_ref, g_ref, b_ref, wr_ref, br_ref,
                    xo_ref, route_ref, *, alpha):
    mix = jnp.dot(y_ref[...], w_ref[...].astype(BF16), preferred_element_type=F32)
    xn = _layer_norm(alpha * x_ref[...] + mix, g_ref[...], b_ref[...])
    xo_ref[...] = xn
    xh, xl = _split_bf16(xn)
    wh, wl = _split_bf16(wr_ref[...])
    lg = (jnp.dot(xh, wh, preferred_element_type=F32) + jnp.dot(xl, wh, preferred_element_type=F32)
          + jnp.dot(xh, wl, preferred_element_type=F32) + br_ref[...])
    route_ref[...] = _route_record(lg)


def _outproj_ln_route(y, w_out, x, ln_g, ln_b, w_r, b_r, l, alpha, tm):
    m, d = x.shape
    tm = min(tm, m)
    const = dict(pipeline_mode=pl.Buffered(1))
    return pl.pallas_call(
        functools.partial(_outproj_kernel, alpha=alpha),
        out_shape=(SDS((m, d), F32), SDS((m, ROUTE_W), F32)),
        grid=(m // tm,),
        in_specs=[pl.BlockSpec((tm, d), lambda i: (i, 0)),
                  pl.BlockSpec((None, d, d), lambda i: (l, 0, 0), **const),
                  pl.BlockSpec((tm, d), lambda i: (i, 0)),
                  pl.BlockSpec((None, 1, d), lambda i: (l, 0, 0)),
                  pl.BlockSpec((None, 1, d), lambda i: (l, 0, 0)),
                  pl.BlockSpec((None, d, ROUTE_W), lambda i: (0, 0, 0)),
                  pl.BlockSpec((None, 1, ROUTE_W), lambda i: (0, 0, 0))],
        out_specs=(pl.BlockSpec((tm, d), lambda i: (i, 0)),
                   pl.BlockSpec((tm, ROUTE_W), lambda i: (i, 0))),
        compiler_params=_cparams(1),
        name="outproj_ln_route",
    )(y, w_out, x, ln_g, ln_b, w_r, b_r)


def _row_copy(src_hbm, row, dst, r, sem):
    return pltpu.make_async_copy(src_hbm.at[pl.ds(row, 1), :], dst.at[pl.ds(r, 1), :], sem)


def _moe_kernel(be_ref, nu_ref, tok_ref, x_hbm, wg_ref, wu_ref, wd_ref, o_ref, xbuf, sem, *, bm):
    i = pl.program_id(0)

    @pl.when(i < nu_ref[0])
    def _():
        base = i * bm

        def issue(r, c):
            _row_copy(x_hbm, tok_ref[base + r], xbuf, r, sem).start()
            return c

        lax.fori_loop(0, bm, issue, 0)

        def wait(r, c):
            _row_copy(x_hbm, 0, xbuf, r, sem).wait()
            return c

        lax.fori_loop(0, bm, wait, 0)

        xb = xbuf[...].astype(BF16)
        g = jnp.dot(xb, wg_ref[...].astype(BF16), preferred_element_type=F32)
        u = jnp.dot(xb, wu_ref[...].astype(BF16), preferred_element_type=F32)
        hid = (g * _sigmoid(g)) * u
        o_ref[...] = jnp.dot(hid.astype(BF16), wd_ref[...].astype(BF16),
                             preferred_element_type=F32).astype(o_ref.dtype)

    @pl.when(i >= nu_ref[0])
    def _():
        o_ref[...] = jnp.zeros(o_ref.shape, o_ref.dtype)


def _moe_ffn(x, block_e, n_used, row_tok, w_gate, w_up, w_down, l, bm):
    t, d = x.shape
    f = w_gate.shape[3]
    n_blocks = row_tok.shape[0] // bm

    def live(i, nu):
        return jnp.minimum(i, nu[0] - 1)

    grid_spec = pltpu.PrefetchScalarGridSpec(
        num_scalar_prefetch=3,
        grid=(n_blocks,),
        in_specs=[pl.BlockSpec(memory_space=pl.ANY),
                  pl.BlockSpec((None, None, d, f), lambda i, be, nu, tok: (l, be[live(i, nu)], 0, 0)),
                  pl.BlockSpec((None, None, d, f), lambda i, be, nu, tok: (l, be[live(i, nu)], 0, 0)),
                  pl.BlockSpec((None, None, f, d), lambda i, be, nu, tok: (l, be[live(i, nu)], 0, 0))],
        out_specs=pl.BlockSpec((bm, d), lambda i, be, nu, tok: (i, 0)),
        scratch_shapes=[pltpu.VMEM((bm, d), F32), pltpu.SemaphoreType.DMA],
    )
    return pl.pallas_call(
        functools.partial(_moe_kernel, bm=bm),
        out_shape=SDS((n_blocks * bm, d), F32),
        grid_spec=grid_spec,
        compiler_params=_cparams(1),
        name="moe_ffn",
    )(block_e, n_used, row_tok, x, w_gate, w_up, w_down)


def _combine_kernel(pos_ref, yb_hbm, x_ref, route_ref, g_ref, b_ref, xo_ref, xbo_ref,
                    buf0, buf1, sems, *, tm, alpha):
    base = pl.program_id(0) * tm

    def issue(r, c):
        t2 = 2 * (base + r)
        _row_copy(yb_hbm, pos_ref[t2], buf0, r, sems.at[0]).start()
        _row_copy(yb_hbm, pos_ref[t2 + 1], buf1, r, sems.at[1]).start()
        return c

    lax.fori_loop(0, tm, issue, 0)

    def wait(r, c):
        _row_copy(yb_hbm, 0, buf0, r, sems.at[0]).wait()
        _row_copy(yb_hbm, 0, buf1, r, sems.at[1]).wait()
        return c

    lax.fori_loop(0, tm, wait, 0)

    rec = route_ref[...]
    z = alpha * x_ref[...] + rec[:, 2:3] * buf0[...] + rec[:, 3:4] * buf1[...]
    xn = _layer_norm(z, g_ref[...], b_ref[...])
    xo_ref[...] = xn
    xbo_ref[...] = xn.astype(BF16)


def _combine_ln(pos, yb, x, route, ln_g, ln_b, l, alpha, tm):
    t, d = x.shape
    tm = min(tm, t)
    grid_spec = pltpu.PrefetchScalarGridSpec(
        num_scalar_prefetch=1,
        grid=(t // tm,),
        in_specs=[pl.BlockSpec(memory_space=pl.ANY),
                  pl.BlockSpec((tm, d), lambda i, pos: (i, 0)),
                  pl.BlockSpec((tm, ROUTE_W), lambda i, pos: (i, 0)),
                  pl.BlockSpec((None, 1, d), lambda i, pos: (l, 0, 0)),
                  pl.BlockSpec((None, 1, d), lambda i, pos: (l, 0, 0))],
        out_specs=(pl.BlockSpec((tm, d), lambda i, pos: (i, 0)),
                   pl.BlockSpec((tm, d), lambda i, pos: (i, 0))),
        scratch_shapes=[pltpu.VMEM((tm, d), F32), pltpu.VMEM((tm, d), F32),
                        pltpu.SemaphoreType.DMA((2,))],
    )
    return pl.pallas_call(
        functools.partial(_combine_kernel, tm=tm, alpha=alpha),
        out_shape=(SDS((t, d), F32), SDS((t, d), BF16)),
        grid_spec=grid_spec,
        compiler_params=_cparams(1),
        name="combine_ln",
    )(pos, yb, x, route, ln_g, ln_b)


def _route_meta(e_ids, bm):
    t = e_ids.shape[0]
    a = t * TOP_K
    flat_e = e_ids.reshape(a)
    onehot = (flat_e[:, None] == jnp.arange(N_EXPERTS, dtype=I32)[None, :]).astype(I32)
    csum = jnp.cumsum(onehot, axis=0)
    rank = jnp.take_along_axis(csum, flat_e[:, None], axis=1)[:, 0] - 1
    counts = csum[-1]
    padded = (counts + bm - 1) // bm * bm
    pend = jnp.cumsum(padded)
    pstart = pend - padded
    dest = (pstart[flat_e] + rank).astype(I32)
    n_blocks = a // bm + N_EXPERTS
    row_tok = jnp.zeros((n_blocks * bm,), I32).at[dest].set(jnp.arange(a, dtype=I32) // TOP_K)
    block_e = jnp.clip(jnp.searchsorted(pend, jnp.arange(n_blocks, dtype=I32) * bm, side="right"),
                       0, N_EXPERTS - 1).astype(I32)
    n_used = (pend[-1:] // bm).astype(I32)
    return dest, row_tok, block_e, n_used


def _swap_halves(w):
    half = w.shape[-1] // 2
    return jnp.concatenate([-w[..., half:], w[..., :half]], axis=-1)


def _prep_layer_weights(w_in_l, w_uq_l, w_ukv_l, w_rg_l, b_rg_l, w_re_l, b_re_l):
    d = w_in_l.shape[0]
    gate0 = DA_QK_W * 2 + DA_V_W + MLA_Q_RANK + MLA_KV_RANK + MLA_ROPE_DIM
    w_gates = w_in_l[:, gate0:].astype(BF16)[None]
    wk = w_in_l[:, gate0 - MLA_ROPE_DIM:gate0]
    w_kr = jnp.concatenate([wk, _swap_halves(wk)], axis=1)
    wq = w_uq_l.reshape(MLA_Q_RANK, MLA_HEADS, MLA_NOPE_DIM + MLA_ROPE_DIM)
    wq_rope = wq[..., MLA_NOPE_DIM:]
    w_uq2 = jnp.concatenate([wq[..., :MLA_NOPE_DIM], wq_rope, _swap_halves(wq_rope)],
                            axis=-1).reshape(MLA_Q_RANK, MLA_HEADS * MLA_QCAT)
    wkv = w_ukv_l.reshape(MLA_KV_RANK, MLA_HEADS, MLA_NOPE_DIM + MLA_V_DIM)
    w_ukv2 = jnp.concatenate([wkv[..., :MLA_NOPE_DIM].reshape(MLA_KV_RANK, -1),
                              wkv[..., MLA_NOPE_DIM:].reshape(MLA_KV_RANK, -1)], axis=1)
    pad = ROUTE_W - N_GROUPS - N_EXPERTS
    w_r = jnp.concatenate([w_rg_l, w_re_l, jnp.zeros((d, pad), F32)], axis=1)[None]
    b_r = jnp.concatenate([b_rg_l, b_re_l, jnp.zeros((pad,), F32)])[None, None]
    return w_gates, w_kr, w_uq2, w_ukv2, w_r, b_r


def _rope_tables(seq):
    inv = ROPE_THETA ** (-jnp.arange(0, MLA_ROPE_DIM, 2, dtype=F32) / MLA_ROPE_DIM)
    ang = jnp.arange(seq, dtype=F32)[:, None] * inv[None, :]
    return jnp.cos(ang), jnp.sin(ang)


def kernel(x, w_in, da_lambda, da_subln_g, mla_q_norm_g, mla_w_uq, mla_kv_norm_g, mla_w_ukv, w_branch_a, w_branch_b, w_out, rel_bias, ln1_g, ln1_b, router_w_group, router_b_group, router_w_expert, router_b_expert, expert_w_gate, expert_w_up, expert_w_down, ln2_g, ln2_b):
    batch, seq, d = x.shape
    depth = w_in.shape[0]
    t = batch * seq
    alpha = (2 * depth) ** 0.25
    main_w = DA_QK_W * 2 + DA_V_W + MLA_Q_RANK + MLA_KV_RANK

    cos, sin = _rope_tables(seq)
    tabk = jnp.concatenate([cos, cos, sin, sin], axis=1)
    q_scale = (MLA_NOPE_DIM + MLA_ROPE_DIM) ** -0.5
    tabq = q_scale * jnp.concatenate([jnp.ones((seq, MLA_NOPE_DIM), F32), tabk], axis=1)

    subln_g = da_subln_g[:, None, :]
    ln1_g3, ln1_b3 = ln1_g[:, None, :], ln1_b[:, None, :]
    ln2_g3, ln2_b3 = ln2_g[:, None, :], ln2_b[:, None, :]

    xf = x.reshape(t, d)
    xb = xf.astype(BF16)
    for l in range(depth):
        lam_init = 0.8 - 0.6 * math.exp(-0.3 * l)
        w_gates, w_kr, w_uq2, w_ukv2, w_r, b_r = _prep_layer_weights(
            w_in[l], mla_w_uq[l], mla_w_ukv[l], router_w_group[l], router_b_group[l],
            router_w_expert[l], router_b_expert[l])

        h_main = _matmul(xb, w_in, l, 0, main_w, 2048, 256, BF16, "mm_main")
        h_g = _matmul(xb, w_gates, 0, 0, 2 * d, 1024, 1024, BF16, "mm_gates")
        kr2 = _krope(xb, w_kr, tabk, seq, 512)
        o_a = _da_attention(h_main, rel_bias, da_lambda, subln_g, l, lam_init, batch, seq)
        q_cat = _uq(h_main, mla_q_norm_g[l][None], w_uq2, tabq, seq, 512)
        kv = _ukv(h_main, mla_kv_norm_g[l][None], w_ukv2, 512)
        o_b = _mla_attention(q_cat, kv, kr2, batch, seq)
        y = _gated(o_a, o_b, w_branch_a, w_branch_b, h_g, l, 1024, 512)
        x1, route = _outproj_ln_route(y, w_out, xf, ln1_g3, ln1_b3, w_r, b_r, l, alpha, 256)

        e_ids = route[:, :TOP_K].astype(I32)
        pos, row_tok, block_e, n_used = _route_meta(e_ids, MOE_BLOCK)
        yb = _moe_ffn(x1, block_e, n_used, row_tok, expert_w_gate, expert_w_up, expert_w_down,
                      l, MOE_BLOCK)
        xf, xb = _combine_ln(pos, yb, x1, route, ln2_g3, ln2_b3, l, alpha, 256)
    return xf.reshape(batch, seq, d)
```

```python
import functools
import math

import jax
import jax.numpy as jnp
from jax import lax
from jax.experimental import pallas as pl
from jax.experimental.pallas import tpu as pltpu

F32 = jnp.float32
BF16 = jnp.bfloat16
I32 = jnp.int32
SDS = jax.ShapeDtypeStruct

DA_HEADS = 8
DA_HEAD_DIM = 64
DA_V_DIM = 2 * DA_HEAD_DIM
MLA_HEADS = 8
MLA_Q_RANK = 768
MLA_KV_RANK = 512
MLA_NOPE_DIM = 128
MLA_ROPE_DIM = 64
MLA_V_DIM = 128
ROPE_THETA = 10000.0
REL_BUCKETS = 32
REL_MAX_DIST = 128
REL_MAX_EXACT = REL_BUCKETS // 2
N_GROUPS = 4
EXPERTS_PER_GROUP = 8
N_EXPERTS = N_GROUPS * EXPERTS_PER_GROUP
TOP_K = 2
NORM_EPS = 1e-5
NEG_INF = -1e30

DA_QK_W = DA_HEADS * 2 * DA_HEAD_DIM
DA_V_W = DA_HEADS * DA_V_DIM
MLA_O_W = MLA_HEADS * MLA_V_DIM
MLA_QCAT = 2 * MLA_NOPE_DIM

LANES = 128
SUBLANES = 8
VMEM_LIMIT = 56 * 1024 * 1024

ATT_K_BLOCK = 512
DA_Q_BLOCK = 256
MOE_BLOCK = 256
ROUTE_W = LANES
GATHER_UNROLL = 8


def _cparams(n_axes):
    return pltpu.CompilerParams(dimension_semantics=("arbitrary",) * n_axes,
                                vmem_limit_bytes=VMEM_LIMIT)


def _dot_nt(a, b):
    return lax.dot_general(a, b, (((1,), (1,)), ((), ())), preferred_element_type=F32)


def _mm_nt_kernel(a_ref, w_ref, o_ref):
    o_ref[...] = _dot_nt(a_ref[...], w_ref[0].astype(BF16)).astype(o_ref.dtype)


def _matmul_nt(a, wt3, l, row0, n_rows, tm, tn, out_dtype, name):
    m, k = a.shape
    tm = min(tm, m)
    assert m % tm == 0 and n_rows % tn == 0
    if row0 % tn == 0:
        r0 = row0 // tn
        w_spec = pl.BlockSpec((1, tn, k), lambda i, j: (l, r0 + j, 0))
    else:
        w_spec = pl.BlockSpec((pl.Element(1), pl.Element(tn), pl.Element(k)),
                              lambda i, j: (l, pl.multiple_of(row0 + j * tn, SUBLANES), 0))
    return pl.pallas_call(
        _mm_nt_kernel,
        out_shape=SDS((m, n_rows), out_dtype),
        grid=(m // tm, n_rows // tn),
        in_specs=[pl.BlockSpec((tm, k), lambda i, j: (i, 0)), w_spec],
        out_specs=pl.BlockSpec((tm, tn), lambda i, j: (i, j)),
        compiler_params=_cparams(2),
        name=name,
    )(a, wt3)


def _krope_kernel(a_ref, w_ref, tab_ref, o_ref):
    t = _dot_nt(a_ref[...], w_ref[...].astype(BF16))
    t = t * tab_ref[...]
    o_ref[...] = (t + pltpu.roll(t, MLA_ROPE_DIM, axis=1)).astype(o_ref.dtype)


def _krope(xb, w_kr, tabk, seq, tm):
    m, k = xb.shape
    tm = min(tm, seq)
    nsb = seq // tm
    return pl.pallas_call(
        _krope_kernel,
        out_shape=SDS((m, LANES), BF16),
        grid=(m // tm,),
        in_specs=[pl.BlockSpec((tm, k), lambda i: (i, 0)),
                  pl.BlockSpec((LANES, k), lambda i: (0, 0)),
                  pl.BlockSpec((tm, LANES), lambda i: (i % nsb, 0))],
        out_specs=pl.BlockSpec((tm, LANES), lambda i: (i, 0)),
        compiler_params=_cparams(1),
        name="krope",
    )(xb, w_kr, tabk)


def _rms(c, g):
    return c * lax.rsqrt(jnp.mean(c * c, axis=-1, keepdims=True) + NORM_EPS) * g


def _uq_kernel(c_ref, g_ref, w_ref, tab_ref, o_ref):
    n = _rms(c_ref[...].astype(F32), g_ref[...])
    acc = jnp.dot(n.astype(BF16), w_ref[...].astype(BF16), preferred_element_type=F32)
    tab = tab_ref[...]
    for h in range(MLA_HEADS):
        sl = slice(h * MLA_QCAT, (h + 1) * MLA_QCAT)
        o_ref[:, sl] = (acc[:, sl] * tab).astype(o_ref.dtype)


def _uq(h_main, g, w_uq2, tabq, seq, tm):
    m = h_main.shape[0]
    tm = min(tm, seq)
    nsb = seq // tm
    cq_blk = (DA_QK_W * 2 + DA_V_W) // MLA_Q_RANK
    n_out = MLA_HEADS * MLA_QCAT
    return pl.pallas_call(
        _uq_kernel,
        out_shape=SDS((m, n_out), BF16),
        grid=(m // tm,),
        in_specs=[pl.BlockSpec((tm, MLA_Q_RANK), lambda i: (i, cq_blk)),
                  pl.BlockSpec((1, MLA_Q_RANK), lambda i: (0, 0)),
                  pl.BlockSpec((MLA_Q_RANK, n_out), lambda i: (0, 0)),
                  pl.BlockSpec((tm, MLA_QCAT), lambda i: (i % nsb, 0))],
        out_specs=pl.BlockSpec((tm, n_out), lambda i: (i, 0)),
        compiler_params=_cparams(1),
        name="mla_uq",
    )(h_main, g, w_uq2, tabq)


def _ukv_kernel(c0_ref, c1_ref, g_ref, w_ref, o_ref):
    c = jnp.concatenate([c0_ref[...], c1_ref[...]], axis=1).astype(F32)
    n = _rms(c, g_ref[...])
    o_ref[...] = jnp.dot(n.astype(BF16), w_ref[...].astype(BF16),
                         preferred_element_type=F32).astype(o_ref.dtype)


def _ukv(h_main, g, w_ukv2, tm):
    m = h_main.shape[0]
    tm = min(tm, m)
    half = MLA_KV_RANK // 2
    b0 = (DA_QK_W * 2 + DA_V_W + MLA_Q_RANK) // half
    n_out = w_ukv2.shape[1]
    return pl.pallas_call(
        _ukv_kernel,
        out_shape=SDS((m, n_out), BF16),
        grid=(m // tm,),
        in_specs=[pl.BlockSpec((tm, half), lambda i: (i, b0)),
                  pl.BlockSpec((tm, half), lambda i: (i, b0 + 1)),
                  pl.BlockSpec((1, MLA_KV_RANK), lambda i: (0, 0)),
                  pl.BlockSpec((MLA_KV_RANK, n_out), lambda i: (0, 0))],
        out_specs=pl.BlockSpec((tm, n_out), lambda i: (i, 0)),
        compiler_params=_cparams(1),
        name="mla_ukv",
    )(h_main, h_main, g, w_ukv2)


def _flash_init(m_scr, acc_scr):
    m_scr[...] = jnp.full(m_scr.shape, NEG_INF, F32)
    acc_scr[...] = jnp.zeros(acc_scr.shape, F32)


def _flash_update(s, v, m_scr, acc_scr):
    v1 = jnp.concatenate([v, jnp.ones(v.shape, v.dtype)], axis=1)
    m_prev = m_scr[...]
    m_new = jnp.maximum(m_prev, jnp.max(s, axis=1, keepdims=True))
    p = jnp.exp(s - jnp.concatenate([m_new] * (s.shape[1] // LANES), axis=1))
    alpha = jnp.exp(m_prev - m_new)
    acc_scr[...] = (jnp.concatenate([alpha, alpha], axis=1) * acc_scr[...]
                    + jnp.dot(p.astype(BF16), v1, preferred_element_type=F32))
    m_scr[...] = m_new


def _flash_result(acc_scr):
    acc = acc_scr[...]
    return acc[:, :LANES] / acc[:, LANES:]


def _t5_bias_blocks(rb_ref, h, bias_scr, tq, tk):
    row = lax.broadcasted_iota(I32, (tq, tk), 0)
    col = lax.broadcasted_iota(I32, (tq, tk), 1)
    far = rb_ref[REL_BUCKETS - 1, h]
    for d in range(bias_scr.shape[0]):
        n = row - col + d * tq
        nn = jnp.maximum(n, 0)
        nf = jnp.maximum(nn, 1).astype(F32)
        large = REL_MAX_EXACT + (jnp.log(nf / REL_MAX_EXACT) / math.log(REL_MAX_DIST / REL_MAX_EXACT)
                                 * (REL_BUCKETS - REL_MAX_EXACT)).astype(I32)
        large = jnp.minimum(large, REL_BUCKETS - 1)
        bucket = jnp.where(nn < REL_MAX_EXACT, nn, large)
        val = jnp.zeros((tq, tk), F32)
        for bb in range(REL_BUCKETS):
            val = jnp.where(bucket == bb, rb_ref[bb, h], val)
        bias_scr[d] = jnp.where(n >= 0, val - far, NEG_INF)


def _da_kernel(rb_ref, lamv_ref, q_ref, k_ref, v_ref, g_ref, o_ref,
               bias_scr, m_scr, acc_scr, *, tq, tk, lam_init):
    h = pl.program_id(0)
    b = pl.program_id(1)
    qi = pl.program_id(2)
    ratio = tk // tq
    jd = qi // ratio
    par = qi % ratio

    @pl.when((b == 0) & (qi == 0))
    def _():
        _t5_bias_blocks(rb_ref, h, bias_scr, tq, tk)

    q = q_ref[...]
    lane = lax.broadcasted_iota(I32, q.shape, 1)
    scale = DA_HEAD_DIM ** -0.5
    zero = jnp.zeros_like(q)
    q2 = jnp.concatenate([jnp.where(lane < DA_HEAD_DIM, q, zero),
                          jnp.where(lane >= DA_HEAD_DIM, q, zero)], axis=0) * scale

    _flash_init(m_scr, acc_scr)

    def block(j, bias):
        off = pl.multiple_of(j * tk, tk)
        s = _dot_nt(q2, k_ref[pl.ds(off, tk), :])
        if bias is not None:
            s = s + jnp.concatenate([bias, bias], axis=0)
        _flash_update(s, v_ref[pl.ds(off, tk), :], m_scr, acc_scr)

    def far_body(j, c):
        block(j, None)
        return c

    lax.fori_loop(0, jd - 1, far_body, 0)

    @pl.when((jd >= 1) & (par == 0))
    def _():
        block(jd - 1, bias_scr[ratio])

    @pl.when((jd >= 1) & (par != 0))
    def _():
        block(jd - 1, None)

    block(jd, bias_scr[par])

    lamv = lamv_ref[...]
    lam = (jnp.exp(jnp.sum(lamv[0:1] * lamv[1:2], axis=1, keepdims=True))
           - jnp.exp(jnp.sum(lamv[2:3] * lamv[3:4], axis=1, keepdims=True)) + lam_init)
    o12 = _flash_result(acc_scr)
    o = o12[:tq] - lam * o12[tq:]
    o = o * lax.rsqrt(jnp.mean(o * o, axis=1, keepdims=True) + NORM_EPS) * g_ref[...] * (1.0 - lam_init)
    o_ref[...] = o.astype(o_ref.dtype)


def _da_attention(h_main, rel_bias, da_lambda, subln_g, l, lam_init, batch, seq):
    tq, tk = DA_Q_BLOCK, ATT_K_BLOCK
    ratio = tk // tq
    assert seq % tk == 0 and tk % tq == 0 and tq + 1 >= REL_MAX_DIST
    nq = seq // tq
    hd = 2 * DA_HEAD_DIM
    kcol = DA_QK_W // hd
    vcol = 2 * DA_QK_W // DA_V_DIM
    return pl.pallas_call(
        functools.partial(_da_kernel, tq=tq, tk=tk, lam_init=lam_init),
        out_shape=SDS((batch * seq, DA_V_W), BF16),
        grid=(DA_HEADS, batch, nq),
        in_specs=[pl.BlockSpec(memory_space=pltpu.SMEM),
                  pl.BlockSpec((None, 4, DA_HEAD_DIM), lambda h, b, qi: (l, 0, 0)),
                  pl.BlockSpec((tq, hd), lambda h, b, qi: (b * nq + qi, h)),
                  pl.BlockSpec((seq, hd), lambda h, b, qi: (b, kcol + h)),
                  pl.BlockSpec((seq, DA_V_DIM), lambda h, b, qi: (b, vcol + h)),
                  pl.BlockSpec((None, 1, DA_V_DIM), lambda h, b, qi: (l, 0, 0))],
        out_specs=pl.BlockSpec((tq, DA_V_DIM), lambda h, b, qi: (b * nq + qi, h)),
        scratch_shapes=[pltpu.VMEM((ratio + 1, tq, tk), F32),
                        pltpu.VMEM((2 * tq, LANES), F32),
                        pltpu.VMEM((2 * tq, 2 * LANES), F32)],
        compiler_params=_cparams(3),
        name="da_attn",
    )(rel_bias, da_lambda, h_main, h_main, h_main, subln_g)


def _mla_kernel(q_ref, kn_ref, kr_ref, v_ref, o_ref, m_scr, acc_scr, *, blk):
    qi = pl.program_id(2)
    q = q_ref[...]
    _flash_init(m_scr, acc_scr)

    def block(j, masked):
        off = pl.multiple_of(j * blk, blk)
        kcat = jnp.concatenate([kn_ref[pl.ds(off, blk), :], kr_ref[pl.ds(off, blk), :]], axis=1)
        s = _dot_nt(q, kcat)
        if masked:
            row = lax.broadcasted_iota(I32, s.shape, 0)
            col = lax.broadcasted_iota(I32, s.shape, 1)
            s = jnp.where(col <= row, s, NEG_INF)
        _flash_update(s, v_ref[pl.ds(off, blk), :], m_scr, acc_scr)

    def far_body(j, c):
        block(j, False)
        return c

    lax.fori_loop(0, qi, far_body, 0)
    block(qi, True)
    o_ref[...] = _flash_result(acc_scr).astype(o_ref.dtype)


def _mla_attention(q_cat, kv, kr2, batch, seq):
    blk = min(ATT_K_BLOCK, seq)
    nq = seq // blk
    return pl.pallas_call(
        functools.partial(_mla_kernel, blk=blk),
        out_shape=SDS((batch * seq, MLA_O_W), BF16),
        grid=(batch, MLA_HEADS, nq),
        in_specs=[pl.BlockSpec((blk, MLA_QCAT), lambda b, h, qi: (b * nq + qi, h)),
                  pl.BlockSpec((seq, MLA_NOPE_DIM), lambda b, h, qi: (b, h)),
                  pl.BlockSpec((seq, LANES), lambda b, h, qi: (b, 0)),
                  pl.BlockSpec((seq, MLA_V_DIM), lambda b, h, qi: (b, MLA_HEADS + h))],
        out_specs=pl.BlockSpec((blk, MLA_V_DIM), lambda b, h, qi: (b * nq + qi, h)),
        scratch_shapes=[pltpu.VMEM((blk, LANES), F32),
                        pltpu.VMEM((blk, 2 * LANES), F32)],
        compiler_params=_cparams(3),
        name="mla_attn",
    )(q_cat, kv, kr2, kv)


def _sigmoid(x):
    return 1.0 / (1.0 + jnp.exp(-x))


def _gated_kernel(oa_ref, ob_ref, wa_ref, wb_ref, ga_ref, gb_ref, o_ref):
    ya = jnp.dot(oa_ref[...], wa_ref[...].astype(BF16), preferred_element_type=F32)
    yb = jnp.dot(ob_ref[...], wb_ref[...].astype(BF16), preferred_element_type=F32)
    y = _sigmoid(ga_ref[...].astype(F32)) * ya + _sigmoid(gb_ref[...].astype(F32)) * yb
    o_ref[...] = y.astype(o_ref.dtype)


def _gated(o_a, o_b, w_a, w_b, h_g, l, tm, tn):
    m = o_a.shape[0]
    d = w_a.shape[2]
    tm = min(tm, m)
    ngb = d // tn
    return pl.pallas_call(
        _gated_kernel,
        out_shape=SDS((m, d), BF16),
        grid=(m // tm, ngb),
        in_specs=[pl.BlockSpec((tm, DA_V_W), lambda i, j: (i, 0)),
                  pl.BlockSpec((tm, MLA_O_W), lambda i, j: (i, 0)),
                  pl.BlockSpec((None, DA_V_W, tn), lambda i, j: (l, 0, j)),
                  pl.BlockSpec((None, MLA_O_W, tn), lambda i, j: (l, 0, j)),
                  pl.BlockSpec((tm, tn), lambda i, j: (i, j)),
                  pl.BlockSpec((tm, tn), lambda i, j: (i, ngb + j))],
        out_specs=pl.BlockSpec((tm, tn), lambda i, j: (i, j)),
        compiler_params=_cparams(2),
        name="gated",
    )(o_a, o_b, w_a, w_b, h_g, h_g)


def _layer_norm(z, g, b):
    mu = jnp.mean(z, axis=-1, keepdims=True)
    zc = z - mu
    var = jnp.mean(zc * zc, axis=-1, keepdims=True)
    return zc * lax.rsqrt(var + NORM_EPS) * g + b


def _split_bf16(x):
    hi = x.astype(BF16)
    lo = (x - hi.astype(F32)).astype(BF16)
    return hi, lo


def _route_record(lg):
    lane = lax.broadcasted_iota(I32, lg.shape, 1)
    lane_f = lane.astype(F32)
    big = float(ROUTE_W)
    gmask = lane < N_GROUPS
    g_max = jnp.max(jnp.where(gmask, lg, NEG_INF), axis=1, keepdims=True)
    g_idx = jnp.min(jnp.where(gmask & (lg == g_max), lane_f, big), axis=1, keepdims=True)
    g_w = 1.0 / jnp.sum(jnp.where(gmask, jnp.exp(lg - g_max), 0.0), axis=1, keepdims=True)

    lo = N_GROUPS + EXPERTS_PER_GROUP * g_idx
    sel = (lane_f >= lo) & (lane_f < lo + EXPERTS_PER_GROUP)
    e_max = jnp.max(jnp.where(sel, lg, NEG_INF), axis=1, keepdims=True)
    pe = jnp.where(sel, jnp.exp(lg - e_max), 0.0)
    prob = pe / jnp.sum(pe, axis=1, keepdims=True)
    p1 = jnp.max(jnp.where(sel, prob, -1.0), axis=1, keepdims=True)
    i1 = jnp.min(jnp.where(sel & (prob == p1), lane_f, big), axis=1, keepdims=True)
    sel2 = sel & (lane_f != i1)
    p2 = jnp.max(jnp.where(sel2, prob, -1.0), axis=1, keepdims=True)
    i2 = jnp.min(jnp.where(sel2 & (prob == p2), lane_f, big), axis=1, keepdims=True)
    tot = p1 + p2
    w1 = g_w * (p1 / tot)
    w2 = g_w * (p2 / tot)
    rec = jnp.where(lane == 0, i1 - N_GROUPS,
          jnp.where(lane == 1, i2 - N_GROUPS,
          jnp.where(lane == 2, w1, jnp.where(lane == 3, w2, 0.0))))
    return rec


def _outproj_kernel(y_ref, w_ref, x_ref, g_ref, b_ref, wr_ref, br_ref,
                    xo_ref, route_ref, *, alpha):
    mix = jnp.dot(y_ref[...], w_ref[...].astype(BF16), preferred_element_type=F32)
    xn = _layer_norm(alpha * x_ref[...] + mix, g_ref[...], b_ref[...])
    xo_ref[...] = xn
    xh, xl = _split_bf16(xn)
    wh, wl = _split_bf16(wr_ref[...])
    lg = (jnp.dot(xh, wh, preferred_element_type=F32) + jnp.dot(xl, wh, preferred_element_type=F32)
          + jnp.dot(xh, wl, preferred_element_type=F32) + br_ref[...])
    route_ref[...] = _route_record(lg)


def _outproj_ln_route(y, w_out, x, ln_g, ln_b, w_r, b_r, l, alpha, tm):
    m, d = x.shape
    tm = min(tm, m)
    const = dict(pipeline_mode=pl.Buffered(1))
    return pl.pallas_call(
        functools.partial(_outproj_kernel, alpha=alpha),
        out_shape=(SDS((m, d), F32), SDS((m, ROUTE_W), F32)),
        grid=(m // tm,),
        in_specs=[pl.BlockSpec((tm, d), lambda i: (i, 0)),
                  pl.BlockSpec((None, d, d), lambda i: (l, 0, 0), **const),
                  pl.BlockSpec((tm, d), lambda i: (i, 0)),
                  pl.BlockSpec((None, 1, d), lambda i: (l, 0, 0)),
                  pl.BlockSpec((None, 1, d), lambda i: (l, 0, 0)),
                  pl.BlockSpec((None, d, ROUTE_W), lambda i: (0, 0, 0)),
                  pl.BlockSpec((None, 1, ROUTE_W), lambda i: (0, 0, 0))],
        out_specs=(pl.BlockSpec((tm, d), lambda i: (i, 0)),
                   pl.BlockSpec((tm, ROUTE_W), lambda i: (i, 0))),
        compiler_params=_cparams(1),
        name="outproj_ln_route",
    )(y, w_out, x, ln_g, ln_b, w_r, b_r)


def _row_copy(src_hbm, row, dst, r, sem):
    return pltpu.make_async_copy(src_hbm.at[pl.ds(row, 1), :], dst.at[pl.ds(r, 1), :], sem)


def _gather_rows(src_hbm, idx_ref, base, dst, sem, n):
    def issue(r, c):
        _row_copy(src_hbm, idx_ref[base + r], dst, r, sem).start()
        return c

    lax.fori_loop(0, n, issue, 0, unroll=GATHER_UNROLL)


def _gather_wait(src_hbm, dst, sem, n):
    def wait(r, c):
        _row_copy(src_hbm, 0, dst, r, sem).wait()
        return c

    lax.fori_loop(0, n, wait, 0, unroll=GATHER_UNROLL)


def _moe_kernel(bs_ref, nb_ref, nu_ref, tok_ref, x_hbm, wg_ref, wu_ref, wd_ref, y_hbm,
                xbuf, obuf, gsem, osem, *, bm, n_blocks):
    e = pl.program_id(0)
    n_used = nu_ref[0]

    def out_copy(g, slot):
        rows = pl.ds(pl.multiple_of(g * bm, bm), bm)
        return pltpu.make_async_copy(obuf.at[slot], y_hbm.at[rows, :], osem.at[slot])

    @pl.when(e == 0)
    def _():
        _gather_rows(x_hbm, tok_ref, 0, xbuf.at[0], gsem.at[0], bm)

    def body(j, c):
        g = bs_ref[e] + j
        slot = lax.rem(g, 2)

        @pl.when(g + 1 < n_used)
        def _():
            _gather_rows(x_hbm, tok_ref, (g + 1) * bm, xbuf.at[1 - slot], gsem.at[1 - slot], bm)

        _gather_wait(x_hbm, xbuf.at[slot], gsem.at[slot], bm)

        @pl.when(g >= 2)
        def _():
            out_copy(g - 2, slot).wait()

        xb = xbuf[slot].astype(BF16)
        gt = _dot_nt(xb, wg_ref[...].astype(BF16))
        up = _dot_nt(xb, wu_ref[...].astype(BF16))
        hid = (gt * _sigmoid(gt)) * up
        obuf[slot] = jnp.dot(hid.astype(BF16), wd_ref[...].astype(BF16), preferred_element_type=F32)
        out_copy(g, slot).start()
        return c

    lax.fori_loop(0, nb_ref[e], body, 0)

    @pl.when(e == pl.num_programs(0) - 1)
    def _():
        @pl.when(n_used >= 2)
        def _():
            out_copy(n_used - 2, lax.rem(n_used - 2, 2)).wait()

        out_copy(n_used - 1, lax.rem(n_used - 1, 2)).wait()

        obuf[0] = jnp.zeros(obuf.shape[1:], obuf.dtype)

        def fill(g, c):
            cp = out_copy(g, 0)
            cp.start()
            cp.wait()
            return c

        lax.fori_loop(n_used, n_blocks, fill, 0)


def _moe_ffn(x, bstart, nblk, n_used, row_tok, wgt, wut, wd, l, bm):
    t, d = x.shape
    f = wd.shape[2]
    n_blocks = row_tok.shape[0] // bm
    w_spec = pl.BlockSpec((None, None, f, d), lambda e, bs, nb, nu, tok: (l, e, 0, 0))
    grid_spec = pltpu.PrefetchScalarGridSpec(
        num_scalar_prefetch=4,
        grid=(N_EXPERTS,),
        in_specs=[pl.BlockSpec(memory_space=pl.ANY), w_spec, w_spec, w_spec],
        out_specs=pl.BlockSpec(memory_space=pl.ANY),
        scratch_shapes=[pltpu.VMEM((2, bm, d), F32), pltpu.VMEM((2, bm, d), F32),
                        pltpu.SemaphoreType.DMA((2,)), pltpu.SemaphoreType.DMA((2,))],
    )
    return pl.pallas_call(
        functools.partial(_moe_kernel, bm=bm, n_blocks=n_blocks),
        out_shape=SDS((n_blocks * bm, d), F32),
        grid_spec=grid_spec,
        compiler_params=_cparams(1),
        name="moe_ffn",
    )(bstart, nblk, n_used, row_tok, x, wgt, wut, wd)


def _combine_kernel(pos_ref, yb_hbm, x_ref, route_ref, g_ref, b_ref, xo_ref, xbo_ref,
                    buf, sems, *, tm, t_total, alpha):
    i = pl.program_id(0)
    slot = lax.rem(i, 2)

    def gather(step, s):
        for k in range(TOP_K):
            _gather_rows(yb_hbm, pos_ref, k * t_total + step * tm, buf.at[s, k],
                         sems.at[s * TOP_K + k], tm)

    @pl.when(i == 0)
    def _():
        gather(0, 0)

    @pl.when(i + 1 < pl.num_programs(0))
    def _():
        gather(i + 1, 1 - slot)

    for k in range(TOP_K):
        _gather_wait(yb_hbm, buf.at[slot, k], sems.at[slot * TOP_K + k], tm)

    rec = route_ref[...]
    z = alpha * x_ref[...] + rec[:, 2:3] * buf[slot, 0] + rec[:, 3:4] * buf[slot, 1]
    xn = _layer_norm(z, g_ref[...], b_ref[...])
    xo_ref[...] = xn
    xbo_ref[...] = xn.astype(BF16)


def _combine_ln(pos_k, yb, x, route, ln_g, ln_b, l, alpha, tm):
    t, d = x.shape
    tm = min(tm, t)
    grid_spec = pltpu.PrefetchScalarGridSpec(
        num_scalar_prefetch=1,
        grid=(t // tm,),
        in_specs=[pl.BlockSpec(memory_space=pl.ANY),
                  pl.BlockSpec((tm, d), lambda i, pos: (i, 0)),
                  pl.BlockSpec((tm, ROUTE_W), lambda i, pos: (i, 0)),
                  pl.BlockSpec((None, 1, d), lambda i, pos: (l, 0, 0)),
                  pl.BlockSpec((None, 1, d), lambda i, pos: (l, 0, 0))],
        out_specs=(pl.BlockSpec((tm, d), lambda i, pos: (i, 0)),
                   pl.BlockSpec((tm, d), lambda i, pos: (i, 0))),
        scratch_shapes=[pltpu.VMEM((2, TOP_K, tm, d), F32),
                        pltpu.SemaphoreType.DMA((2 * TOP_K,))],
    )
    return pl.pallas_call(
        functools.partial(_combine_kernel, tm=tm, t_total=t, alpha=alpha),
        out_shape=(SDS((t, d), F32), SDS((t, d), BF16)),
        grid_spec=grid_spec,
        compiler_params=_cparams(1),
        name="combine_ln",
    )(pos_k, yb, x, route, ln_g, ln_b)


def _route_meta(e_ids, bm):
    t = e_ids.shape[0]
    a = t * TOP_K
    flat_e = e_ids.reshape(a)
    onehot = (flat_e[:, None] == jnp.arange(N_EXPERTS, dtype=I32)[None, :]).astype(I32)
    csum = jnp.cumsum(onehot, axis=0)
    rank = jnp.take_along_axis(csum, flat_e[:, None], axis=1)[:, 0] - 1
    counts = csum[-1]
    nblk = (counts + bm - 1) // bm
    bend = jnp.cumsum(nblk)
    bstart = bend - nblk
    dest = (bstart[flat_e] * bm + rank).astype(I32)
    n_blocks = a // bm + N_EXPERTS
    row_tok = jnp.zeros((n_blocks * bm,), I32).at[dest].set(jnp.arange(a, dtype=I32) // TOP_K)
    pos_k = dest.reshape(t, TOP_K).T.reshape(a)
    return pos_k, row_tok, bstart.astype(I32), nblk.astype(I32), bend[-1:].astype(I32)


def _swap_halves(w, axis):
    half = w.shape[axis] // 2
    lo = lax.slice_in_dim(w, 0, half, axis=axis)
    hi = lax.slice_in_dim(w, half, 2 * half, axis=axis)
    return jnp.concatenate([-hi, lo], axis=axis)


def _prep_layer_weights(wk_t, w_uq_l, w_ukv_l, w_rg_l, b_rg_l, w_re_l, b_re_l):
    d = wk_t.shape[1]
    w_krt = jnp.concatenate([wk_t, _swap_halves(wk_t, 0)], axis=0)
    wq = w_uq_l.reshape(MLA_Q_RANK, MLA_HEADS, MLA_NOPE_DIM + MLA_ROPE_DIM)
    wq_rope = wq[..., MLA_NOPE_DIM:]
    w_uq2 = jnp.concatenate([wq[..., :MLA_NOPE_DIM], wq_rope, _swap_halves(wq_rope, 2)],
                            axis=-1).reshape(MLA_Q_RANK, MLA_HEADS * MLA_QCAT)
    wkv = w_ukv_l.reshape(MLA_KV_RANK, MLA_HEADS, MLA_NOPE_DIM + MLA_V_DIM)
    w_ukv2 = jnp.concatenate([wkv[..., :MLA_NOPE_DIM].reshape(MLA_KV_RANK, -1),
                              wkv[..., MLA_NOPE_DIM:].reshape(MLA_KV_RANK, -1)], axis=1)
    pad = ROUTE_W - N_GROUPS - N_EXPERTS
    w_r = jnp.concatenate([w_rg_l, w_re_l, jnp.zeros((d, pad), F32)], axis=1)[None]
    b_r = jnp.concatenate([b_rg_l, b_re_l, jnp.zeros((pad,), F32)])[None, None]
    return w_krt, w_uq2, w_ukv2, w_r, b_r


def _rope_tables(seq):
    inv = ROPE_THETA ** (-jnp.arange(0, MLA_ROPE_DIM, 2, dtype=F32) / MLA_ROPE_DIM)
    ang = jnp.arange(seq, dtype=F32)[:, None] * inv[None, :]
    return jnp.cos(ang), jnp.sin(ang)


def kernel(x, w_in, da_lambda, da_subln_g, mla_q_norm_g, mla_w_uq, mla_kv_norm_g, mla_w_ukv, w_branch_a, w_branch_b, w_out, rel_bias, ln1_g, ln1_b, router_w_group, router_b_group, router_w_expert, router_b_expert, expert_w_gate, expert_w_up, expert_w_down, ln2_g, ln2_b):
    batch, seq, d = x.shape
    depth = w_in.shape[0]
    t = batch * seq
    alpha = (2 * depth) ** 0.25
    main_w = DA_QK_W * 2 + DA_V_W + MLA_Q_RANK + MLA_KV_RANK
    gate0 = main_w + MLA_ROPE_DIM

    cos, sin = _rope_tables(seq)
    tabk = jnp.concatenate([cos, cos, sin, sin], axis=1)
    q_scale = (MLA_NOPE_DIM + MLA_ROPE_DIM) ** -0.5
    tabq = q_scale * jnp.concatenate([jnp.ones((seq, MLA_NOPE_DIM), F32), tabk], axis=1)

    subln_g = da_subln_g[:, None, :]
    ln1_g3, ln1_b3 = ln1_g[:, None, :], ln1_b[:, None, :]
    ln2_g3, ln2_b3 = ln2_g[:, None, :], ln2_b[:, None, :]

    wt_in = jnp.swapaxes(w_in, 1, 2)
    wgt = jnp.swapaxes(expert_w_gate, 2, 3)
    wut = jnp.swapaxes(expert_w_up, 2, 3)

    xf = x.reshape(t, d)
    xb = xf.astype(BF16)
    for l in range(depth):
        lam_init = 0.8 - 0.6 * math.exp(-0.3 * l)
        w_krt, w_uq2, w_ukv2, w_r, b_r = _prep_layer_weights(
            wt_in[l, main_w:gate0, :], mla_w_uq[l], mla_w_ukv[l], router_w_group[l],
            router_b_group[l], router_w_expert[l], router_b_expert[l])

        h_main = _matmul_nt(xb, wt_in, l, 0, main_w, 2048, 256, BF16, "mm_main")
        h_g = _matmul_nt(xb, wt_in, l, gate0, 2 * d, 1024, 512, BF16, "mm_gates")
        kr2 = _krope(xb, w_krt, tabk, seq, 512)
        o_a = _da_attention(h_main, rel_bias, da_lambda, subln_g, l, lam_init, batch, seq)
        q_cat = _uq(h_main, mla_q_norm_g[l][None], w_uq2, tabq, seq, 512)
        kv = _ukv(h_main, mla_kv_norm_g[l][None], w_ukv2, 512)
        o_b = _mla_attention(q_cat, kv, kr2, batch, seq)
        y = _gated(o_a, o_b, w_branch_a, w_branch_b, h_g, l, 1024, 512)
        x1, route = _outproj_ln_route(y, w_out, xf, ln1_g3, ln1_b3, w_r, b_r, l, alpha, 256)

        e_ids = route[:, :TOP_K].astype(I32)
        pos_k, row_tok, bstart, nblk, n_used = _route_meta(e_ids, MOE_BLOCK)
        yb = _moe_ffn(x1, bstart, nblk, n_used, row_tok, wgt, wut, expert_w_down, l, MOE_BLOCK)
        xf, xb = _combine_ln(pos_k, yb, x1, route, ln2_g3, ln2_b3, l, alpha, 256)
    return xf.reshape(batch, seq, d)
```

```python
import functools
import math

import jax
import jax.numpy as jnp
from jax import lax
from jax.experimental import pallas as pl
from jax.experimental.pallas import tpu as pltpu

F32 = jnp.float32
BF16 = jnp.bfloat16
I32 = jnp.int32
SDS = jax.ShapeDtypeStruct

DA_HEADS = 8
DA_HEAD_DIM = 64
DA_V_DIM = 2 * DA_HEAD_DIM
MLA_HEADS = 8
MLA_Q_RANK = 768
MLA_KV_RANK = 512
MLA_NOPE_DIM = 128
MLA_ROPE_DIM = 64
MLA_V_DIM = 128
ROPE_THETA = 10000.0
REL_BUCKETS = 32
REL_MAX_DIST = 128
REL_MAX_EXACT = REL_BUCKETS // 2
N_GROUPS = 4
EXPERTS_PER_GROUP = 8
N_EXPERTS = N_GROUPS * EXPERTS_PER_GROUP
TOP_K = 2
NORM_EPS = 1e-5
NEG_INF = -1e30

DA_QK_W = DA_HEADS * 2 * DA_HEAD_DIM
DA_V_W = DA_HEADS * DA_V_DIM
MLA_O_W = MLA_HEADS * MLA_V_DIM
MLA_QCAT = 2 * MLA_NOPE_DIM

LANES = 128
SUBLANES = 8
VMEM_LIMIT = 56 * 1024 * 1024

ATT_K_BLOCK = 512
DA_Q_BLOCK = 256
MOE_BLOCK = 256
ROUTE_W = LANES
GATHER_UNROLL = 8
HEADS_PER_STEP = 4
GATHER_DMA_PRIORITY = 1


def _cparams(n_axes):
    return pltpu.CompilerParams(dimension_semantics=("arbitrary",) * n_axes,
                                vmem_limit_bytes=VMEM_LIMIT)


def _dot_nt(a, b):
    return lax.dot_general(a, b, (((1,), (1,)), ((), ())), preferred_element_type=F32)


def _mm_nt_kernel(a_ref, w_ref, o_ref):
    o_ref[...] = _dot_nt(a_ref[...], w_ref[0].astype(BF16)).astype(o_ref.dtype)


def _matmul_nt(a, wt3, l, row0, n_rows, tm, tn, out_dtype, name):
    m, k = a.shape
    tm = min(tm, m)
    assert m % tm == 0 and n_rows % tn == 0
    if row0 % tn == 0:
        r0 = row0 // tn
        w_spec = pl.BlockSpec((1, tn, k), lambda i, j: (l, r0 + j, 0))
    else:
        w_spec = pl.BlockSpec((pl.Element(1), pl.Element(tn), pl.Element(k)),
                              lambda i, j: (l, pl.multiple_of(row0 + j * tn, SUBLANES), 0))
    return pl.pallas_call(
        _mm_nt_kernel,
        out_shape=SDS((m, n_rows), out_dtype),
        grid=(m // tm, n_rows // tn),
        in_specs=[pl.BlockSpec((tm, k), lambda i, j: (i, 0)), w_spec],
        out_specs=pl.BlockSpec((tm, tn), lambda i, j: (i, j)),
        compiler_params=_cparams(2),
        name=name,
    )(a, wt3)


def _krope_kernel(a_ref, w_ref, tab_ref, o_ref):
    t = _dot_nt(a_ref[...], w_ref[...].astype(BF16))
    t = t * tab_ref[...]
    o_ref[...] = (t + pltpu.roll(t, MLA_ROPE_DIM, axis=1)).astype(o_ref.dtype)


def _krope(xb, w_kr, tabk, seq, tm):
    m, k = xb.shape
    tm = min(tm, seq)
    nsb = seq // tm
    return pl.pallas_call(
        _krope_kernel,
        out_shape=SDS((m, LANES), BF16),
        grid=(m // tm,),
        in_specs=[pl.BlockSpec((tm, k), lambda i: (i, 0)),
                  pl.BlockSpec((LANES, k), lambda i: (0, 0)),
                  pl.BlockSpec((tm, LANES), lambda i: (i % nsb, 0))],
        out_specs=pl.BlockSpec((tm, LANES), lambda i: (i, 0)),
        compiler_params=_cparams(1),
        name="krope",
    )(xb, w_kr, tabk)


def _rms(c, g):
    return c * lax.rsqrt(jnp.mean(c * c, axis=-1, keepdims=True) + NORM_EPS) * g


def _uq_kernel(c_ref, g_ref, w_ref, tab_ref, o_ref):
    n = _rms(c_ref[...].astype(F32), g_ref[...])
    acc = jnp.dot(n.astype(BF16), w_ref[...].astype(BF16), preferred_element_type=F32)
    tab = tab_ref[...]
    for h in range(MLA_HEADS):
        sl = slice(h * MLA_QCAT, (h + 1) * MLA_QCAT)
        o_ref[:, sl] = (acc[:, sl] * tab).astype(o_ref.dtype)


def _uq(h_main, g, w_uq2, tabq, seq, tm):
    m = h_main.shape[0]
    tm = min(tm, seq)
    nsb = seq // tm
    cq_blk = (DA_QK_W * 2 + DA_V_W) // MLA_Q_RANK
    n_out = MLA_HEADS * MLA_QCAT
    return pl.pallas_call(
        _uq_kernel,
        out_shape=SDS((m, n_out), BF16),
        grid=(m // tm,),
        in_specs=[pl.BlockSpec((tm, MLA_Q_RANK), lambda i: (i, cq_blk)),
                  pl.BlockSpec((1, MLA_Q_RANK), lambda i: (0, 0)),
                  pl.BlockSpec((MLA_Q_RANK, n_out), lambda i: (0, 0)),
                  pl.BlockSpec((tm, MLA_QCAT), lambda i: (i % nsb, 0))],
        out_specs=pl.BlockSpec((tm, n_out), lambda i: (i, 0)),
        compiler_params=_cparams(1),
        name="mla_uq",
    )(h_main, g, w_uq2, tabq)


def _ukv_kernel(c0_ref, c1_ref, g_ref, w_ref, o_ref):
    c = jnp.concatenate([c0_ref[...], c1_ref[...]], axis=1).astype(F32)
    n = _rms(c, g_ref[...])
    o_ref[...] = jnp.dot(n.astype(BF16), w_ref[...].astype(BF16),
                         preferred_element_type=F32).astype(o_ref.dtype)


def _ukv(h_main, g, w_ukv2, tm):
    m = h_main.shape[0]
    tm = min(tm, m)
    half = MLA_KV_RANK // 2
    b0 = (DA_QK_W * 2 + DA_V_W + MLA_Q_RANK) // half
    n_out = w_ukv2.shape[1]
    return pl.pallas_call(
        _ukv_kernel,
        out_shape=SDS((m, n_out), BF16),
        grid=(m // tm,),
        in_specs=[pl.BlockSpec((tm, half), lambda i: (i, b0)),
                  pl.BlockSpec((tm, half), lambda i: (i, b0 + 1)),
                  pl.BlockSpec((1, MLA_KV_RANK), lambda i: (0, 0)),
                  pl.BlockSpec((MLA_KV_RANK, n_out), lambda i: (0, 0))],
        out_specs=pl.BlockSpec((tm, n_out), lambda i: (i, 0)),
        compiler_params=_cparams(1),
        name="mla_ukv",
    )(h_main, h_main, g, w_ukv2)


def _flash_init(m_scr, acc_scr):
    m_scr[...] = jnp.full(m_scr.shape, NEG_INF, F32)
    acc_scr[...] = jnp.zeros(acc_scr.shape, F32)


def _flash_update(s, v, m_scr, acc_scr):
    v1 = jnp.concatenate([v, jnp.ones(v.shape, v.dtype)], axis=1)
    m_prev = m_scr[...]
    m_new = jnp.maximum(m_prev, jnp.max(s, axis=1, keepdims=True))
    p = jnp.exp(s - jnp.concatenate([m_new] * (s.shape[1] // LANES), axis=1))
    alpha = jnp.exp(m_prev - m_new)
    acc_scr[...] = (jnp.concatenate([alpha, alpha], axis=1) * acc_scr[...]
                    + jnp.dot(p.astype(BF16), v1, preferred_element_type=F32))
    m_scr[...] = m_new


def _flash_result(acc_scr):
    acc = acc_scr[...]
    return acc[:, :LANES] / acc[:, LANES:]


def _t5_bias_blocks(rb_ref, h, bias_scr, tq, tk):
    row = lax.broadcasted_iota(I32, (tq, tk), 0)
    col = lax.broadcasted_iota(I32, (tq, tk), 1)
    far = rb_ref[REL_BUCKETS - 1, h]
    for d in range(bias_scr.shape[0]):
        n = row - col + d * tq
        nn = jnp.maximum(n, 0)
        nf = jnp.maximum(nn, 1).astype(F32)
        large = REL_MAX_EXACT + (jnp.log(nf / REL_MAX_EXACT) / math.log(REL_MAX_DIST / REL_MAX_EXACT)
                                 * (REL_BUCKETS - REL_MAX_EXACT)).astype(I32)
        large = jnp.minimum(large, REL_BUCKETS - 1)
        bucket = jnp.where(nn < REL_MAX_EXACT, nn, large)
        val = jnp.zeros((tq, tk), F32)
        for bb in range(REL_BUCKETS):
            val = jnp.where(bucket == bb, rb_ref[bb, h], val)
        bias_scr[d] = jnp.where(n >= 0, val - far, NEG_INF)


def _da_kernel(rb_ref, lamv_ref, q_ref, k_ref, v_ref, g_ref, o_ref,
               bias_scr, m_scr, acc_scr, *, tq, tk, lam_init):
    hp = pl.program_id(0)
    b = pl.program_id(1)
    qi = pl.program_id(2)
    ratio = tk // tq
    jd = qi // ratio
    par = qi % ratio
    hd = 2 * DA_HEAD_DIM
    heads = range(HEADS_PER_STEP)

    @pl.when((b == 0) & (qi == 0))
    def _():
        for hh in heads:
            _t5_bias_blocks(rb_ref, hp * HEADS_PER_STEP + hh, bias_scr.at[hh], tq, tk)

    scale = DA_HEAD_DIM ** -0.5
    q_all = q_ref[...]
    q2 = []
    for hh in heads:
        q = q_all[:, hh * hd:(hh + 1) * hd]
        lane = lax.broadcasted_iota(I32, q.shape, 1)
        zero = jnp.zeros_like(q)
        q2.append(jnp.concatenate([jnp.where(lane < DA_HEAD_DIM, q, zero),
                                   jnp.where(lane >= DA_HEAD_DIM, q, zero)], axis=0) * scale)
        _flash_init(m_scr.at[hh], acc_scr.at[hh])

    def block(j, bias_idx):
        off = pl.multiple_of(j * tk, tk)
        kb = k_ref[pl.ds(off, tk), :]
        vb = v_ref[pl.ds(off, tk), :]
        for hh in heads:
            s = _dot_nt(q2[hh], kb[:, hh * hd:(hh + 1) * hd])
            if bias_idx is not None:
                bias = bias_scr[hh, bias_idx]
                s = s + jnp.concatenate([bias, bias], axis=0)
            _flash_update(s, vb[:, hh * DA_V_DIM:(hh + 1) * DA_V_DIM], m_scr.at[hh], acc_scr.at[hh])

    def far_body(j, c):
        block(j, None)
        return c

    lax.fori_loop(0, jd - 1, far_body, 0)

    @pl.when((jd >= 1) & (par == 0))
    def _():
        block(jd - 1, ratio)

    @pl.when((jd >= 1) & (par != 0))
    def _():
        block(jd - 1, None)

    block(jd, par)

    lamv = lamv_ref[...]
    lam = (jnp.exp(jnp.sum(lamv[0:1] * lamv[1:2], axis=1, keepdims=True))
           - jnp.exp(jnp.sum(lamv[2:3] * lamv[3:4], axis=1, keepdims=True)) + lam_init)
    for hh in heads:
        o12 = _flash_result(acc_scr.at[hh])
        o = o12[:tq] - lam * o12[tq:]
        o = o * lax.rsqrt(jnp.mean(o * o, axis=1, keepdims=True) + NORM_EPS) * g_ref[...] * (1.0 - lam_init)
        o_ref[:, hh * DA_V_DIM:(hh + 1) * DA_V_DIM] = o.astype(o_ref.dtype)


def _da_attention(h_main, rel_bias, da_lambda, subln_g, l, lam_init, batch, seq):
    tq, tk = DA_Q_BLOCK, ATT_K_BLOCK
    ratio = tk // tq
    assert seq % tk == 0 and tk % tq == 0 and tq + 1 >= REL_MAX_DIST
    nq = seq // tq
    hw = HEADS_PER_STEP * 2 * DA_HEAD_DIM
    vw = HEADS_PER_STEP * DA_V_DIM
    kcol = DA_QK_W // hw
    vcol = 2 * DA_QK_W // vw
    return pl.pallas_call(
        functools.partial(_da_kernel, tq=tq, tk=tk, lam_init=lam_init),
        out_shape=SDS((batch * seq, DA_V_W), BF16),
        grid=(DA_HEADS // HEADS_PER_STEP, batch, nq),
        in_specs=[pl.BlockSpec(memory_space=pltpu.SMEM),
                  pl.BlockSpec((None, 4, DA_HEAD_DIM), lambda h, b, qi: (l, 0, 0)),
                  pl.BlockSpec((tq, hw), lambda h, b, qi: (b * nq + qi, h)),
                  pl.BlockSpec((seq, hw), lambda h, b, qi: (b, kcol + h)),
                  pl.BlockSpec((seq, vw), lambda h, b, qi: (b, vcol + h)),
                  pl.BlockSpec((None, 1, DA_V_DIM), lambda h, b, qi: (l, 0, 0))],
        out_specs=pl.BlockSpec((tq, vw), lambda h, b, qi: (b * nq + qi, h)),
        scratch_shapes=[pltpu.VMEM((HEADS_PER_STEP, ratio + 1, tq, tk), F32),
                        pltpu.VMEM((HEADS_PER_STEP, 2 * tq, LANES), F32),
                        pltpu.VMEM((HEADS_PER_STEP, 2 * tq, 2 * LANES), F32)],
        compiler_params=_cparams(3),
        name="da_attn",
    )(rel_bias, da_lambda, h_main, h_main, h_main, subln_g)


def _mla_kernel(q_ref, kn_ref, kr_ref, v_ref, o_ref, m_scr, acc_scr, *, blk):
    qi = pl.program_id(2)
    heads = range(HEADS_PER_STEP)
    q_all = q_ref[...]
    q = [q_all[:, hh * MLA_QCAT:(hh + 1) * MLA_QCAT] for hh in heads]
    for hh in heads:
        _flash_init(m_scr.at[hh], acc_scr.at[hh])

    def block(j, masked):
        off = pl.multiple_of(j * blk, blk)
        kn = kn_ref[pl.ds(off, blk), :]
        kr = kr_ref[pl.ds(off, blk), :]
        vb = v_ref[pl.ds(off, blk), :]
        for hh in heads:
            kcat = jnp.concatenate([kn[:, hh * MLA_NOPE_DIM:(hh + 1) * MLA_NOPE_DIM], kr], axis=1)
            s = _dot_nt(q[hh], kcat)
            if masked:
                row = lax.broadcasted_iota(I32, s.shape, 0)
                col = lax.broadcasted_iota(I32, s.shape, 1)
                s = jnp.where(col <= row, s, NEG_INF)
            _flash_update(s, vb[:, hh * MLA_V_DIM:(hh + 1) * MLA_V_DIM], m_scr.at[hh], acc_scr.at[hh])

    def far_body(j, c):
        block(j, False)
        return c

    lax.fori_loop(0, qi, far_body, 0)
    block(qi, True)
    for hh in heads:
        o_ref[:, hh * MLA_V_DIM:(hh + 1) * MLA_V_DIM] = _flash_result(acc_scr.at[hh]).astype(o_ref.dtype)


def _mla_attention(q_cat, kv, kr2, batch, seq):
    blk = min(ATT_K_BLOCK, seq)
    nq = seq // blk
    hps = HEADS_PER_STEP
    vcol = MLA_HEADS // hps
    return pl.pallas_call(
        functools.partial(_mla_kernel, blk=blk),
        out_shape=SDS((batch * seq, MLA_O_W), BF16),
        grid=(batch, MLA_HEADS // hps, nq),
        in_specs=[pl.BlockSpec((blk, hps * MLA_QCAT), lambda b, h, qi: (b * nq + qi, h)),
                  pl.BlockSpec((seq, hps * MLA_NOPE_DIM), lambda b, h, qi: (b, h)),
                  pl.BlockSpec((seq, LANES), lambda b, h, qi: (b, 0)),
                  pl.BlockSpec((seq, hps * MLA_V_DIM), lambda b, h, qi: (b, vcol + h))],
        out_specs=pl.BlockSpec((blk, hps * MLA_V_DIM), lambda b, h, qi: (b * nq + qi, h)),
        scratch_shapes=[pltpu.VMEM((hps, blk, LANES), F32),
                        pltpu.VMEM((hps, blk, 2 * LANES), F32)],
        compiler_params=_cparams(3),
        name="mla_attn",
    )(q_cat, kv, kr2, kv)


def _sigmoid(x):
    return 1.0 / (1.0 + jnp.exp(-x))


def _gated_kernel(oa_ref, ob_ref, wa_ref, wb_ref, ga_ref, gb_ref, o_ref):
    ya = jnp.dot(oa_ref[...], wa_ref[...].astype(BF16), preferred_element_type=F32)
    yb = jnp.dot(ob_ref[...], wb_ref[...].astype(BF16), preferred_element_type=F32)
    y = _sigmoid(ga_ref[...].astype(F32)) * ya + _sigmoid(gb_ref[...].astype(F32)) * yb
    o_ref[...] = y.astype(o_ref.dtype)


def _gated(o_a, o_b, w_a, w_b, h_g, l, tm, tn):
    m = o_a.shape[0]
    d = w_a.shape[2]
    tm = min(tm, m)
    ngb = d // tn
    return pl.pallas_call(
        _gated_kernel,
        out_shape=SDS((m, d), BF16),
        grid=(m // tm, ngb),
        in_specs=[pl.BlockSpec((tm, DA_V_W), lambda i, j: (i, 0)),
                  pl.BlockSpec((tm, MLA_O_W), lambda i, j: (i, 0)),
                  pl.BlockSpec((None, DA_V_W, tn), lambda i, j: (l, 0, j)),
                  pl.BlockSpec((None, MLA_O_W, tn), lambda i, j: (l, 0, j)),
                  pl.BlockSpec((tm, tn), lambda i, j: (i, j)),
                  pl.BlockSpec((tm, tn), lambda i, j: (i, ngb + j))],
        out_specs=pl.BlockSpec((tm, tn), lambda i, j: (i, j)),
        compiler_params=_cparams(2),
        name="gated",
    )(o_a, o_b, w_a, w_b, h_g, h_g)


def _layer_norm(z, g, b):
    mu = jnp.mean(z, axis=-1, keepdims=True)
    zc = z - mu
    var = jnp.mean(zc * zc, axis=-1, keepdims=True)
    return zc * lax.rsqrt(var + NORM_EPS) * g + b


def _split_bf16(x):
    hi = x.astype(BF16)
    lo = (x - hi.astype(F32)).astype(BF16)
    return hi, lo


def _route_record(lg):
    lane = lax.broadcasted_iota(I32, lg.shape, 1)
    lane_f = lane.astype(F32)
    big = float(ROUTE_W)
    gmask = lane < N_GROUPS
    g_max = jnp.max(jnp.where(gmask, lg, NEG_INF), axis=1, keepdims=True)
    g_idx = jnp.min(jnp.where(gmask & (lg == g_max), lane_f, big), axis=1, keepdims=True)
    g_w = 1.0 / jnp.sum(jnp.where(gmask, jnp.exp(lg - g_max), 0.0), axis=1, keepdims=True)

    lo = N_GROUPS + EXPERTS_PER_GROUP * g_idx
    sel = (lane_f >= lo) & (lane_f < lo + EXPERTS_PER_GROUP)
    e_max = jnp.max(jnp.where(sel, lg, NEG_INF), axis=1, keepdims=True)
    pe = jnp.where(sel, jnp.exp(lg - e_max), 0.0)
    prob = pe / jnp.sum(pe, axis=1, keepdims=True)
    p1 = jnp.max(jnp.where(sel, prob, -1.0), axis=1, keepdims=True)
    i1 = jnp.min(jnp.where(sel & (prob == p1), lane_f, big), axis=1, keepdims=True)
    sel2 = sel & (lane_f != i1)
    p2 = jnp.max(jnp.where(sel2, prob, -1.0), axis=1, keepdims=True)
    i2 = jnp.min(jnp.where(sel2 & (prob == p2), lane_f, big), axis=1, keepdims=True)
    tot = p1 + p2
    w1 = g_w * (p1 / tot)
    w2 = g_w * (p2 / tot)
    rec = jnp.where(lane == 0, i1 - N_GROUPS,
          jnp.where(lane == 1, i2 - N_GROUPS,
          jnp.where(lane == 2, w1, jnp.where(lane == 3, w2, 0.0))))
    return rec


def _outproj_kernel(y_ref, w_ref, x_ref, g_ref, b_ref, wr_ref, br_ref,
                    xo_ref, route_ref, *, alpha):
    mix = jnp.dot(y_ref[...], w_ref[...].astype(BF16), preferred_element_type=F32)
    xn = _layer_norm(alpha * x_ref[...] + mix, g_ref[...], b_ref[...])
    xo_ref[...] = xn
    xh, xl = _split_bf16(xn)
    wh, wl = _split_bf16(wr_ref[...])
    lg = (jnp.dot(xh, wh, preferred_element_type=F32) + jnp.dot(xl, wh, preferred_element_type=F32)
          + jnp.dot(xh, wl, preferred_element_type=F32) + br_ref[...])
    route_ref[...] = _route_record(lg)


def _outproj_ln_route(y, w_out, x, ln_g, ln_b, w_r, b_r, l, alpha, tm):
    m, d = x.shape
    tm = min(tm, m)
    const = dict(pipeline_mode=pl.Buffered(1))
    return pl.pallas_call(
        functools.partial(_outproj_kernel, alpha=alpha),
        out_shape=(SDS((m, d), F32), SDS((m, ROUTE_W), F32)),
        grid=(m // tm,),
        in_specs=[pl.BlockSpec((tm, d), lambda i: (i, 0)),
                  pl.BlockSpec((None, d, d), lambda i: (l, 0, 0), **const),
                  pl.BlockSpec((tm, d), lambda i: (i, 0)),
                  pl.BlockSpec((None, 1, d), lambda i: (l, 0, 0)),
                  pl.BlockSpec((None, 1, d), lambda i: (l, 0, 0)),
                  pl.BlockSpec((None, d, ROUTE_W), lambda i: (0, 0, 0)),
                  pl.BlockSpec((None, 1, ROUTE_W), lambda i: (0, 0, 0))],
        out_specs=(pl.BlockSpec((tm, d), lambda i: (i, 0)),
                   pl.BlockSpec((tm, ROUTE_W), lambda i: (i, 0))),
        compiler_params=_cparams(1),
        name="outproj_ln_route",
    )(y, w_out, x, ln_g, ln_b, w_r, b_r)


def _row_copy(src_hbm, row, dst, r, sem):
    return pltpu.make_async_copy(src_hbm.at[pl.ds(row, 1), :], dst.at[pl.ds(r, 1), :], sem)


def _gather_rows(src_hbm, idx_ref, base, dst, sem, n):
    def issue(r, c):
        _row_copy(src_hbm, idx_ref[base + r], dst, r, sem).start(priority=GATHER_DMA_PRIORITY)
        return c

    lax.fori_loop(0, n, issue, 0, unroll=GATHER_UNROLL)


def _gather_wait(src_hbm, dst, sem, n):
    def wait(r, c):
        _row_copy(src_hbm, 0, dst, r, sem).wait()
        return c

    lax.fori_loop(0, n, wait, 0, unroll=GATHER_UNROLL)


def _moe_kernel(bs_ref, nb_ref, nu_ref, tok_ref, x_hbm, wg_ref, wu_ref, wd_ref, y_hbm,
                xbuf, obuf, gsem, osem, *, bm, n_blocks):
    e = pl.program_id(0)
    n_used = nu_ref[0]

    def out_copy(g, slot):
        rows = pl.ds(pl.multiple_of(g * bm, bm), bm)
        return pltpu.make_async_copy(obuf.at[slot], y_hbm.at[rows, :], osem.at[slot])

    @pl.when(e == 0)
    def _():
        _gather_rows(x_hbm, tok_ref, 0, xbuf.at[0], gsem.at[0], bm)

    def body(j, c):
        g = bs_ref[e] + j
        slot = lax.rem(g, 2)

        @pl.when(g + 1 < n_used)
        def _():
            _gather_rows(x_hbm, tok_ref, (g + 1) * bm, xbuf.at[1 - slot], gsem.at[1 - slot], bm)

        _gather_wait(x_hbm, xbuf.at[slot], gsem.at[slot], bm)

        @pl.when(g >= 2)
        def _():
            out_copy(g - 2, slot).wait()

        xb = xbuf[slot].astype(BF16)
        gt = _dot_nt(xb, wg_ref[...].astype(BF16))
        up = _dot_nt(xb, wu_ref[...].astype(BF16))
        hid = (gt * _sigmoid(gt)) * up
        obuf[slot] = jnp.dot(hid.astype(BF16), wd_ref[...].astype(BF16), preferred_element_type=F32)
        out_copy(g, slot).start()
        return c

    lax.fori_loop(0, nb_ref[e], body, 0)

    @pl.when(e == pl.num_programs(0) - 1)
    def _():
        @pl.when(n_used >= 2)
        def _():
            out_copy(n_used - 2, lax.rem(n_used - 2, 2)).wait()

        out_copy(n_used - 1, lax.rem(n_used - 1, 2)).wait()

        obuf[0] = jnp.zeros(obuf.shape[1:], obuf.dtype)

        def fill(g, c):
            cp = out_copy(g, 0)
            cp.start()
            cp.wait()
            return c

        lax.fori_loop(n_used, n_blocks, fill, 0)


def _moe_ffn(x, bstart, nblk, n_used, row_tok, wgt, wut, wd, l, bm):
    t, d = x.shape
    f = wd.shape[2]
    n_blocks = row_tok.shape[0] // bm
    w_spec = pl.BlockSpec((None, None, f, d), lambda e, bs, nb, nu, tok: (l, e, 0, 0))
    grid_spec = pltpu.PrefetchScalarGridSpec(
        num_scalar_prefetch=4,
        grid=(N_EXPERTS,),
        in_specs=[pl.BlockSpec(memory_space=pl.ANY), w_spec, w_spec, w_spec],
        out_specs=pl.BlockSpec(memory_space=pl.ANY),
        scratch_shapes=[pltpu.VMEM((2, bm, d), F32), pltpu.VMEM((2, bm, d), F32),
                        pltpu.SemaphoreType.DMA((2,)), pltpu.SemaphoreType.DMA((2,))],
    )
    return pl.pallas_call(
        functools.partial(_moe_kernel, bm=bm, n_blocks=n_blocks),
        out_shape=SDS((n_blocks * bm, d), F32),
        grid_spec=grid_spec,
        compiler_params=_cparams(1),
        name="moe_ffn",
    )(bstart, nblk, n_used, row_tok, x, wgt, wut, wd)


def _combine_kernel(pos_ref, yb_hbm, x_ref, route_ref, g_ref, b_ref, xo_ref, xbo_ref,
                    buf, sems, *, tm, t_total, alpha):
    i = pl.program_id(0)
    slot = lax.rem(i, 2)

    def gather(step, s):
        for k in range(TOP_K):
            _gather_rows(yb_hbm, pos_ref, k * t_total + step * tm, buf.at[s, k],
                         sems.at[s * TOP_K + k], tm)

    @pl.when(i == 0)
    def _():
        gather(0, 0)

    @pl.when(i + 1 < pl.num_programs(0))
    def _():
        gather(i + 1, 1 - slot)

    for k in range(TOP_K):
        _gather_wait(yb_hbm, buf.at[slot, k], sems.at[slot * TOP_K + k], tm)

    rec = route_ref[...]
    z = alpha * x_ref[...] + rec[:, 2:3] * buf[slot, 0] + rec[:, 3:4] * buf[slot, 1]
    xn = _layer_norm(z, g_ref[...], b_ref[...])
    xo_ref[...] = xn
    xbo_ref[...] = xn.astype(BF16)


def _combine_ln(pos_k, yb, x, route, ln_g, ln_b, l, alpha, tm):
    t, d = x.shape
    tm = min(tm, t)
    grid_spec = pltpu.PrefetchScalarGridSpec(
        num_scalar_prefetch=1,
        grid=(t // tm,),
        in_specs=[pl.BlockSpec(memory_space=pl.ANY),
                  pl.BlockSpec((tm, d), lambda i, pos: (i, 0)),
                  pl.BlockSpec((tm, ROUTE_W), lambda i, pos: (i, 0)),
                  pl.BlockSpec((None, 1, d), lambda i, pos: (l, 0, 0)),
                  pl.BlockSpec((None, 1, d), lambda i, pos: (l, 0, 0))],
        out_specs=(pl.BlockSpec((tm, d), lambda i, pos: (i, 0)),
                   pl.BlockSpec((tm, d), lambda i, pos: (i, 0))),
        scratch_shapes=[pltpu.VMEM((2, TOP_K, tm, d), F32),
                        pltpu.SemaphoreType.DMA((2 * TOP_K,))],
    )
    return pl.pallas_call(
        functools.partial(_combine_kernel, tm=tm, t_total=t, alpha=alpha),
        out_shape=(SDS((t, d), F32), SDS((t, d), BF16)),
        grid_spec=grid_spec,
        compiler_params=_cparams(1),
        name="combine_ln",
    )(pos_k, yb, x, route, ln_g, ln_b)


def _route_meta(e_ids, bm):
    t = e_ids.shape[0]
    a = t * TOP_K
    flat_e = e_ids.reshape(a)
    onehot = (flat_e[:, None] == jnp.arange(N_EXPERTS, dtype=I32)[None, :]).astype(I32)
    csum = jnp.cumsum(onehot, axis=0)
    rank = jnp.take_along_axis(csum, flat_e[:, None], axis=1)[:, 0] - 1
    counts = csum[-1]
    nblk = (counts + bm - 1) // bm
    bend = jnp.cumsum(nblk)
    bstart = bend - nblk
    dest = (bstart[flat_e] * bm + rank).astype(I32)
    n_blocks = a // bm + N_EXPERTS
    row_tok = jnp.zeros((n_blocks * bm,), I32).at[dest].set(jnp.arange(a, dtype=I32) // TOP_K)
    pos_k = dest.reshape(t, TOP_K).T.reshape(a)
    return pos_k, row_tok, bstart.astype(I32), nblk.astype(I32), bend[-1:].astype(I32)


def _swap_halves(w, axis):
    half = w.shape[axis] // 2
    lo = lax.slice_in_dim(w, 0, half, axis=axis)
    hi = lax.slice_in_dim(w, half, 2 * half, axis=axis)
    return jnp.concatenate([-hi, lo], axis=axis)


def _prep_layer_weights(wk_t, w_uq_l, w_ukv_l, w_rg_l, b_rg_l, w_re_l, b_re_l):
    d = wk_t.shape[1]
    w_krt = jnp.concatenate([wk_t, _swap_halves(wk_t, 0)], axis=0)
    wq = w_uq_l.reshape(MLA_Q_RANK, MLA_HEADS, MLA_NOPE_DIM + MLA_ROPE_DIM)
    wq_rope = wq[..., MLA_NOPE_DIM:]
    w_uq2 = jnp.concatenate([wq[..., :MLA_NOPE_DIM], wq_rope, _swap_halves(wq_rope, 2)],
                            axis=-1).reshape(MLA_Q_RANK, MLA_HEADS * MLA_QCAT)
    wkv = w_ukv_l.reshape(MLA_KV_RANK, MLA_HEADS, MLA_NOPE_DIM + MLA_V_DIM)
    w_ukv2 = jnp.concatenate([wkv[..., :MLA_NOPE_DIM].reshape(MLA_KV_RANK, -1),
                              wkv[..., MLA_NOPE_DIM:].reshape(MLA_KV_RANK, -1)], axis=1)
    pad = ROUTE_W - N_GROUPS - N_EXPERTS
    w_r = jnp.concatenate([w_rg_l, w_re_l, jnp.zeros((d, pad), F32)], axis=1)[None]
    b_r = jnp.concatenate([b_rg_l, b_re_l, jnp.zeros((pad,), F32)])[None, None]
    return w_krt, w_uq2, w_ukv2, w_r, b_r


def _rope_tables(seq):
    inv = ROPE_THETA ** (-jnp.arange(0, MLA_ROPE_DIM, 2, dtype=F32) / MLA_ROPE_DIM)
    ang = jnp.arange(seq, dtype=F32)[:, None] * inv[None, :]
    return jnp.cos(ang), jnp.sin(ang)


def kernel(x, w_in, da_lambda, da_subln_g, mla_q_norm_g, mla_w_uq, mla_kv_norm_g, mla_w_ukv, w_branch_a, w_branch_b, w_out, rel_bias, ln1_g, ln1_b, router_w_group, router_b_group, router_w_expert, router_b_expert, expert_w_gate, expert_w_up, expert_w_down, ln2_g, ln2_b):
    batch, seq, d = x.shape
    depth = w_in.shape[0]
    t = batch * seq
    alpha = (2 * depth) ** 0.25
    main_w = DA_QK_W * 2 + DA_V_W + MLA_Q_RANK + MLA_KV_RANK
    gate0 = main_w + MLA_ROPE_DIM

    cos, sin = _rope_tables(seq)
    tabk = jnp.concatenate([cos, cos, sin, sin], axis=1)
    q_scale = (MLA_NOPE_DIM + MLA_ROPE_DIM) ** -0.5
    tabq = q_scale * jnp.concatenate([jnp.ones((seq, MLA_NOPE_DIM), F32), tabk], axis=1)

    subln_g = da_subln_g[:, None, :]
    ln1_g3, ln1_b3 = ln1_g[:, None, :], ln1_b[:, None, :]
    ln2_g3, ln2_b3 = ln2_g[:, None, :], ln2_b[:, None, :]

    wt_in = jnp.swapaxes(w_in, 1, 2)
    wgt = jnp.swapaxes(expert_w_gate, 2, 3)
    wut = jnp.swapaxes(expert_w_up, 2, 3)

    xf = x.reshape(t, d)
    xb = xf.astype(BF16)
    for l in range(depth):
        lam_init = 0.8 - 0.6 * math.exp(-0.3 * l)
        w_krt, w_uq2, w_ukv2, w_r, b_r = _prep_layer_weights(
            wt_in[l, main_w:gate0, :], mla_w_uq[l], mla_w_ukv[l], router_w_group[l],
            router_b_group[l], router_w_expert[l], router_b_expert[l])

        h_main = _matmul_nt(xb, wt_in, l, 0, main_w, 2048, 256, BF16, "mm_main")
        h_g = _matmul_nt(xb, wt_in, l, gate0, 2 * d, 1024, 512, BF16, "mm_gates")
        kr2 = _krope(xb, w_krt, tabk, seq, 512)
        o_a = _da_attention(h_main, rel_bias, da_lambda, subln_g, l, lam_init, batch, seq)
        q_cat = _uq(h_main, mla_q_norm_g[l][None], w_uq2, tabq, seq, 512)
        kv = _ukv(h_main, mla_kv_norm_g[l][None], w_ukv2, 512)
        o_b = _mla_attention(q_cat, kv, kr2, batch, seq)
        y = _gated(o_a, o_b, w_branch_a, w_branch_b, h_g, l, 1024, 512)
        x1, route = _outproj_ln_route(y, w_out, xf, ln1_g3, ln1_b3, w_r, b_r, l, alpha, 256)

        e_ids = route[:, :TOP_K].astype(I32)
        pos_k, row_tok, bstart, nblk, n_used = _route_meta(e_ids, MOE_BLOCK)
        yb = _moe_ffn(x1, bstart, nblk, n_used, row_tok, wgt, wut, expert_w_down, l, MOE_BLOCK)
        xf, xb = _combine_ln(pos_k, yb, x1, route, ln2_g3, ln2_b3, l, alpha, 256)
    return xf.reshape(batch, seq, d)
```

```python
import functools
import math

import jax
import jax.numpy as jnp
from jax import lax
from jax.experimental import pallas as pl
from jax.experimental.pallas import tpu as pltpu

F32 = jnp.float32
BF16 = jnp.bfloat16
I32 = jnp.int32
SDS = jax.ShapeDtypeStruct

DA_HEADS = 8
DA_HEAD_DIM = 64
DA_V_DIM = 2 * DA_HEAD_DIM
MLA_HEADS = 8
MLA_Q_RANK = 768
MLA_KV_RANK = 512
MLA_NOPE_DIM = 128
MLA_ROPE_DIM = 64
MLA_V_DIM = 128
ROPE_THETA = 10000.0
REL_BUCKETS = 32
REL_MAX_DIST = 128
REL_MAX_EXACT = REL_BUCKETS // 2
N_GROUPS = 4
EXPERTS_PER_GROUP = 8
N_EXPERTS = N_GROUPS * EXPERTS_PER_GROUP
TOP_K = 2
NORM_EPS = 1e-5
NEG_INF = -1e30

DA_QK_W = DA_HEADS * 2 * DA_HEAD_DIM
DA_V_W = DA_HEADS * DA_V_DIM
MLA_O_W = MLA_HEADS * MLA_V_DIM
MLA_QCAT = 2 * MLA_NOPE_DIM

LANES = 128
SUBLANES = 8
VMEM_LIMIT = 56 * 1024 * 1024

ATT_K_BLOCK = 512
DA_Q_BLOCK = 256
MOE_BLOCK = 256
ROUTE_W = LANES
GATHER_UNROLL = 8
ROW_TILES = 16
HEADS_PER_STEP = 4
GATHER_DMA_PRIORITY = 1


def _cparams(n_axes):
    return pltpu.CompilerParams(dimension_semantics=("arbitrary",) * n_axes,
                                vmem_limit_bytes=VMEM_LIMIT)


def _dot_nt(a, b):
    return lax.dot_general(a, b, (((1,), (1,)), ((), ())), preferred_element_type=F32)


def _mm_nt_kernel(a_ref, w_ref, o_ref):
    o_ref[...] = _dot_nt(a_ref[...], w_ref[0].astype(BF16)).astype(o_ref.dtype)


def _matmul_nt(a, wt3, l, row0, n_rows, tm, tn, out_dtype, name):
    m, k = a.shape
    tm = min(tm, m)
    assert m % tm == 0 and n_rows % tn == 0
    if row0 % tn == 0:
        r0 = row0 // tn
        w_spec = pl.BlockSpec((1, tn, k), lambda i, j: (l, r0 + j, 0))
    else:
        w_spec = pl.BlockSpec((pl.Element(1), pl.Element(tn), pl.Element(k)),
                              lambda i, j: (l, pl.multiple_of(row0 + j * tn, SUBLANES), 0))
    return pl.pallas_call(
        _mm_nt_kernel,
        out_shape=SDS((m, n_rows), out_dtype),
        grid=(m // tm, n_rows // tn),
        in_specs=[pl.BlockSpec((tm, k), lambda i, j: (i, 0)), w_spec],
        out_specs=pl.BlockSpec((tm, tn), lambda i, j: (i, j)),
        compiler_params=_cparams(2),
        name=name,
    )(a, wt3)


def _krope_kernel(a_ref, w_ref, tab_ref, o_ref):
    t = _dot_nt(a_ref[...], w_ref[...].astype(BF16))
    t = t * tab_ref[...]
    o_ref[...] = (t + pltpu.roll(t, MLA_ROPE_DIM, axis=1)).astype(o_ref.dtype)


def _krope(xb, w_kr, tabk, seq, tm):
    m, k = xb.shape
    tm = min(tm, seq)
    nsb = seq // tm
    return pl.pallas_call(
        _krope_kernel,
        out_shape=SDS((m, LANES), BF16),
        grid=(m // tm,),
        in_specs=[pl.BlockSpec((tm, k), lambda i: (i, 0)),
                  pl.BlockSpec((LANES, k), lambda i: (0, 0)),
                  pl.BlockSpec((tm, LANES), lambda i: (i % nsb, 0))],
        out_specs=pl.BlockSpec((tm, LANES), lambda i: (i, 0)),
        compiler_params=_cparams(1),
        name="krope",
    )(xb, w_kr, tabk)


def _rms(c, g):
    return c * lax.rsqrt(jnp.mean(c * c, axis=-1, keepdims=True) + NORM_EPS) * g


def _uq_kernel(c_ref, g_ref, w_ref, tab_ref, o_ref):
    n = _rms(c_ref[...].astype(F32), g_ref[...])
    acc = jnp.dot(n.astype(BF16), w_ref[...].astype(BF16), preferred_element_type=F32)
    tab = tab_ref[...]
    for h in range(MLA_HEADS):
        sl = slice(h * MLA_QCAT, (h + 1) * MLA_QCAT)
        o_ref[:, sl] = (acc[:, sl] * tab).astype(o_ref.dtype)


def _uq(h_main, g, w_uq2, tabq, seq, tm):
    m = h_main.shape[0]
    tm = min(tm, seq)
    nsb = seq // tm
    cq_blk = (DA_QK_W * 2 + DA_V_W) // MLA_Q_RANK
    n_out = MLA_HEADS * MLA_QCAT
    return pl.pallas_call(
        _uq_kernel,
        out_shape=SDS((m, n_out), BF16),
        grid=(m // tm,),
        in_specs=[pl.BlockSpec((tm, MLA_Q_RANK), lambda i: (i, cq_blk)),
                  pl.BlockSpec((1, MLA_Q_RANK), lambda i: (0, 0)),
                  pl.BlockSpec((MLA_Q_RANK, n_out), lambda i: (0, 0)),
                  pl.BlockSpec((tm, MLA_QCAT), lambda i: (i % nsb, 0))],
        out_specs=pl.BlockSpec((tm, n_out), lambda i: (i, 0)),
        compiler_params=_cparams(1),
        name="mla_uq",
    )(h_main, g, w_uq2, tabq)


def _ukv_kernel(c0_ref, c1_ref, g_ref, w_ref, o_ref):
    c = jnp.concatenate([c0_ref[...], c1_ref[...]], axis=1).astype(F32)
    n = _rms(c, g_ref[...])
    o_ref[...] = jnp.dot(n.astype(BF16), w_ref[...].astype(BF16),
                         preferred_element_type=F32).astype(o_ref.dtype)


def _ukv(h_main, g, w_ukv2, tm):
    m = h_main.shape[0]
    tm = min(tm, m)
    half = MLA_KV_RANK // 2
    b0 = (DA_QK_W * 2 + DA_V_W + MLA_Q_RANK) // half
    n_out = w_ukv2.shape[1]
    return pl.pallas_call(
        _ukv_kernel,
        out_shape=SDS((m, n_out), BF16),
        grid=(m // tm,),
        in_specs=[pl.BlockSpec((tm, half), lambda i: (i, b0)),
                  pl.BlockSpec((tm, half), lambda i: (i, b0 + 1)),
                  pl.BlockSpec((1, MLA_KV_RANK), lambda i: (0, 0)),
                  pl.BlockSpec((MLA_KV_RANK, n_out), lambda i: (0, 0))],
        out_specs=pl.BlockSpec((tm, n_out), lambda i: (i, 0)),
        compiler_params=_cparams(1),
        name="mla_ukv",
    )(h_main, h_main, g, w_ukv2)


def _flash_init(m_scr, acc_scr):
    m_scr[...] = jnp.full(m_scr.shape, NEG_INF, F32)
    acc_scr[...] = jnp.zeros(acc_scr.shape, F32)


def _flash_update(s, v, m_scr, acc_scr):
    v1 = jnp.concatenate([v, jnp.ones(v.shape, v.dtype)], axis=1)
    m_prev = m_scr[...]
    m_new = jnp.maximum(m_prev, jnp.max(s, axis=1, keepdims=True))
    p = jnp.exp(s - jnp.concatenate([m_new] * (s.shape[1] // LANES), axis=1))
    alpha = jnp.exp(m_prev - m_new)
    acc_scr[...] = (jnp.concatenate([alpha, alpha], axis=1) * acc_scr[...]
                    + jnp.dot(p.astype(BF16), v1, preferred_element_type=F32))
    m_scr[...] = m_new


def _flash_result(acc_scr):
    acc = acc_scr[...]
    return acc[:, :LANES] / acc[:, LANES:]


def _t5_bias_blocks(rb_ref, h, bias_scr, tq, tk):
    row = lax.broadcasted_iota(I32, (tq, tk), 0)
    col = lax.broadcasted_iota(I32, (tq, tk), 1)
    far = rb_ref[REL_BUCKETS - 1, h]
    for d in range(bias_scr.shape[0]):
        n = row - col + d * tq
        nn = jnp.maximum(n, 0)
        nf = jnp.maximum(nn, 1).astype(F32)
        large = REL_MAX_EXACT + (jnp.log(nf / REL_MAX_EXACT) / math.log(REL_MAX_DIST / REL_MAX_EXACT)
                                 * (REL_BUCKETS - REL_MAX_EXACT)).astype(I32)
        large = jnp.minimum(large, REL_BUCKETS - 1)
        bucket = jnp.where(nn < REL_MAX_EXACT, nn, large)
        val = jnp.zeros((tq, tk), F32)
        for bb in range(REL_BUCKETS):
            val = jnp.where(bucket == bb, rb_ref[bb, h], val)
        bias_scr[d] = jnp.where(n >= 0, val - far, NEG_INF)


def _da_kernel(rb_ref, lamv_ref, q_ref, k_ref, v_ref, g_ref, o_ref,
               bias_scr, m_scr, acc_scr, *, tq, tk, lam_init):
    hp = pl.program_id(0)
    b = pl.program_id(1)
    qi = pl.program_id(2)
    ratio = tk // tq
    jd = qi // ratio
    par = qi % ratio
    hd = 2 * DA_HEAD_DIM
    heads = range(HEADS_PER_STEP)

    @pl.when((b == 0) & (qi == 0))
    def _():
        for hh in heads:
            _t5_bias_blocks(rb_ref, hp * HEADS_PER_STEP + hh, bias_scr.at[hh], tq, tk)

    scale = DA_HEAD_DIM ** -0.5
    q_all = q_ref[...]
    q2 = []
    for hh in heads:
        q = q_all[:, hh * hd:(hh + 1) * hd]
        lane = lax.broadcasted_iota(I32, q.shape, 1)
        zero = jnp.zeros_like(q)
        q2.append(jnp.concatenate([jnp.where(lane < DA_HEAD_DIM, q, zero),
                                   jnp.where(lane >= DA_HEAD_DIM, q, zero)], axis=0) * scale)
        _flash_init(m_scr.at[hh], acc_scr.at[hh])

    def block(j, bias_idx):
        off = pl.multiple_of(j * tk, tk)
        kb = k_ref[pl.ds(off, tk), :]
        vb = v_ref[pl.ds(off, tk), :]
        for hh in heads:
            s = _dot_nt(q2[hh], kb[:, hh * hd:(hh + 1) * hd])
            if bias_idx is not None:
                bias = bias_scr[hh, bias_idx]
                s = s + jnp.concatenate([bias, bias], axis=0)
            _flash_update(s, vb[:, hh * DA_V_DIM:(hh + 1) * DA_V_DIM], m_scr.at[hh], acc_scr.at[hh])

    def far_body(j, c):
        block(j, None)
        return c

    lax.fori_loop(0, jd - 1, far_body, 0)

    @pl.when((jd >= 1) & (par == 0))
    def _():
        block(jd - 1, ratio)

    @pl.when((jd >= 1) & (par != 0))
    def _():
        block(jd - 1, None)

    block(jd, par)

    lamv = lamv_ref[...]
    lam = (jnp.exp(jnp.sum(lamv[0:1] * lamv[1:2], axis=1, keepdims=True))
           - jnp.exp(jnp.sum(lamv[2:3] * lamv[3:4], axis=1, keepdims=True)) + lam_init)
    for hh in heads:
        o12 = _flash_result(acc_scr.at[hh])
        o = o12[:tq] - lam * o12[tq:]
        o = o * lax.rsqrt(jnp.mean(o * o, axis=1, keepdims=True) + NORM_EPS) * g_ref[...] * (1.0 - lam_init)
        o_ref[:, hh * DA_V_DIM:(hh + 1) * DA_V_DIM] = o.astype(o_ref.dtype)


def _da_attention(h_main, rel_bias, da_lambda, subln_g, l, lam_init, batch, seq):
    tq, tk = DA_Q_BLOCK, ATT_K_BLOCK
    ratio = tk // tq
    assert seq % tk == 0 and tk % tq == 0 and tq + 1 >= REL_MAX_DIST
    nq = seq // tq
    hw = HEADS_PER_STEP * 2 * DA_HEAD_DIM
    vw = HEADS_PER_STEP * DA_V_DIM
    kcol = DA_QK_W // hw
    vcol = 2 * DA_QK_W // vw
    return pl.pallas_call(
        functools.partial(_da_kernel, tq=tq, tk=tk, lam_init=lam_init),
        out_shape=SDS((batch * seq, DA_V_W), BF16),
        grid=(DA_HEADS // HEADS_PER_STEP, batch, nq),
        in_specs=[pl.BlockSpec(memory_space=pltpu.SMEM),
                  pl.BlockSpec((None, 4, DA_HEAD_DIM), lambda h, b, qi: (l, 0, 0)),
                  pl.BlockSpec((tq, hw), lambda h, b, qi: (b * nq + qi, h)),
                  pl.BlockSpec((seq, hw), lambda h, b, qi: (b, kcol + h)),
                  pl.BlockSpec((seq, vw), lambda h, b, qi: (b, vcol + h)),
                  pl.BlockSpec((None, 1, DA_V_DIM), lambda h, b, qi: (l, 0, 0))],
        out_specs=pl.BlockSpec((tq, vw), lambda h, b, qi: (b * nq + qi, h)),
        scratch_shapes=[pltpu.VMEM((HEADS_PER_STEP, ratio + 1, tq, tk), F32),
                        pltpu.VMEM((HEADS_PER_STEP, 2 * tq, LANES), F32),
                        pltpu.VMEM((HEADS_PER_STEP, 2 * tq, 2 * LANES), F32)],
        compiler_params=_cparams(3),
        name="da_attn",
    )(rel_bias, da_lambda, h_main, h_main, h_main, subln_g)


def _mla_kernel(q_ref, kn_ref, kr_ref, v_ref, o_ref, m_scr, acc_scr, *, blk):
    qi = pl.program_id(2)
    heads = range(HEADS_PER_STEP)
    q_all = q_ref[...]
    q = [q_all[:, hh * MLA_QCAT:(hh + 1) * MLA_QCAT] for hh in heads]
    for hh in heads:
        _flash_init(m_scr.at[hh], acc_scr.at[hh])

    def block(j, masked):
        off = pl.multiple_of(j * blk, blk)
        kn = kn_ref[pl.ds(off, blk), :]
        kr = kr_ref[pl.ds(off, blk), :]
        vb = v_ref[pl.ds(off, blk), :]
        for hh in heads:
            kcat = jnp.concatenate([kn[:, hh * MLA_NOPE_DIM:(hh + 1) * MLA_NOPE_DIM], kr], axis=1)
            s = _dot_nt(q[hh], kcat)
            if masked:
                row = lax.broadcasted_iota(I32, s.shape, 0)
                col = lax.broadcasted_iota(I32, s.shape, 1)
                s = jnp.where(col <= row, s, NEG_INF)
            _flash_update(s, vb[:, hh * MLA_V_DIM:(hh + 1) * MLA_V_DIM], m_scr.at[hh], acc_scr.at[hh])

    def far_body(j, c):
        block(j, False)
        return c

    lax.fori_loop(0, qi, far_body, 0)
    block(qi, True)
    for hh in heads:
        o_ref[:, hh * MLA_V_DIM:(hh + 1) * MLA_V_DIM] = _flash_result(acc_scr.at[hh]).astype(o_ref.dtype)


def _mla_attention(q_cat, kv, kr2, batch, seq):
    blk = min(ATT_K_BLOCK, seq)
    nq = seq // blk
    hps = HEADS_PER_STEP
    vcol = MLA_HEADS // hps
    return pl.pallas_call(
        functools.partial(_mla_kernel, blk=blk),
        out_shape=SDS((batch * seq, MLA_O_W), BF16),
        grid=(batch, MLA_HEADS // hps, nq),
        in_specs=[pl.BlockSpec((blk, hps * MLA_QCAT), lambda b, h, qi: (b * nq + qi, h)),
                  pl.BlockSpec((seq, hps * MLA_NOPE_DIM), lambda b, h, qi: (b, h)),
                  pl.BlockSpec((seq, LANES), lambda b, h, qi: (b, 0)),
                  pl.BlockSpec((seq, hps * MLA_V_DIM), lambda b, h, qi: (b, vcol + h))],
        out_specs=pl.BlockSpec((blk, hps * MLA_V_DIM), lambda b, h, qi: (b * nq + qi, h)),
        scratch_shapes=[pltpu.VMEM((hps, blk, LANES), F32),
                        pltpu.VMEM((hps, blk, 2 * LANES), F32)],
        compiler_params=_cparams(3),
        name="mla_attn",
    )(q_cat, kv, kr2, kv)


def _sigmoid(x):
    return 1.0 / (1.0 + jnp.exp(-x))


def _gated_kernel(oa_ref, ob_ref, wa_ref, wb_ref, ga_ref, gb_ref, o_ref):
    ya = jnp.dot(oa_ref[...], wa_ref[...].astype(BF16), preferred_element_type=F32)
    yb = jnp.dot(ob_ref[...], wb_ref[...].astype(BF16), preferred_element_type=F32)
    y = _sigmoid(ga_ref[...].astype(F32)) * ya + _sigmoid(gb_ref[...].astype(F32)) * yb
    o_ref[...] = y.astype(o_ref.dtype)


def _gated(o_a, o_b, w_a, w_b, h_g, l, tm, tn):
    m = o_a.shape[0]
    d = w_a.shape[2]
    tm = min(tm, m)
    ngb = d // tn
    return pl.pallas_call(
        _gated_kernel,
        out_shape=SDS((m, d), BF16),
        grid=(m // tm, ngb),
        in_specs=[pl.BlockSpec((tm, DA_V_W), lambda i, j: (i, 0)),
                  pl.BlockSpec((tm, MLA_O_W), lambda i, j: (i, 0)),
                  pl.BlockSpec((None, DA_V_W, tn), lambda i, j: (l, 0, j)),
                  pl.BlockSpec((None, MLA_O_W, tn), lambda i, j: (l, 0, j)),
                  pl.BlockSpec((tm, tn), lambda i, j: (i, j)),
                  pl.BlockSpec((tm, tn), lambda i, j: (i, ngb + j))],
        out_specs=pl.BlockSpec((tm, tn), lambda i, j: (i, j)),
        compiler_params=_cparams(2),
        name="gated",
    )(o_a, o_b, w_a, w_b, h_g, h_g)


def _layer_norm(z, g, b):
    mu = jnp.mean(z, axis=-1, keepdims=True)
    zc = z - mu
    var = jnp.mean(zc * zc, axis=-1, keepdims=True)
    return zc * lax.rsqrt(var + NORM_EPS) * g + b


def _split_bf16(x):
    hi = x.astype(BF16)
    lo = (x - hi.astype(F32)).astype(BF16)
    return hi, lo


def _route_record(lg):
    lane = lax.broadcasted_iota(I32, lg.shape, 1)
    lane_f = lane.astype(F32)
    big = float(ROUTE_W)
    gmask = lane < N_GROUPS
    g_max = jnp.max(jnp.where(gmask, lg, NEG_INF), axis=1, keepdims=True)
    g_idx = jnp.min(jnp.where(gmask & (lg == g_max), lane_f, big), axis=1, keepdims=True)
    g_w = 1.0 / jnp.sum(jnp.where(gmask, jnp.exp(lg - g_max), 0.0), axis=1, keepdims=True)

    lo = N_GROUPS + EXPERTS_PER_GROUP * g_idx
    sel = (lane_f >= lo) & (lane_f < lo + EXPERTS_PER_GROUP)
    e_max = jnp.max(jnp.where(sel, lg, NEG_INF), axis=1, keepdims=True)
    pe = jnp.where(sel, jnp.exp(lg - e_max), 0.0)
    prob = pe / jnp.sum(pe, axis=1, keepdims=True)
    p1 = jnp.max(jnp.where(sel, prob, -1.0), axis=1, keepdims=True)
    i1 = jnp.min(jnp.where(sel & (prob == p1), lane_f, big), axis=1, keepdims=True)
    sel2 = sel & (lane_f != i1)
    p2 = jnp.max(jnp.where(sel2, prob, -1.0), axis=1, keepdims=True)
    i2 = jnp.min(jnp.where(sel2 & (prob == p2), lane_f, big), axis=1, keepdims=True)
    tot = p1 + p2
    w1 = g_w * (p1 / tot)
    w2 = g_w * (p2 / tot)
    rec = jnp.where(lane == 0, i1 - N_GROUPS,
          jnp.where(lane == 1, i2 - N_GROUPS,
          jnp.where(lane == 2, w1, jnp.where(lane == 3, w2, 0.0))))
    return rec


def _outproj_kernel(y_ref, w_ref, x_ref, g_ref, b_ref, wr_ref, br_ref,
                    xo_ref, route_ref, *, alpha):
    mix = jnp.dot(y_ref[...], w_ref[...].astype(BF16), preferred_element_type=F32)
    xn = _layer_norm(alpha * x_ref[...] + mix, g_ref[...], b_ref[...])
    xo_ref[...] = xn
    xh, xl = _split_bf16(xn)
    wh, wl = _split_bf16(wr_ref[...])
    lg = (jnp.dot(xh, wh, preferred_element_type=F32) + jnp.dot(xl, wh, preferred_element_type=F32)
          + jnp.dot(xh, wl, preferred_element_type=F32) + br_ref[...])
    route_ref[...] = _route_record(lg)


def _outproj_ln_route(y, w_out, x, ln_g, ln_b, w_r, b_r, l, alpha, tm):
    m, d = x.shape
    tm = min(tm, m)
    const = dict(pipeline_mode=pl.Buffered(1))
    return pl.pallas_call(
        functools.partial(_outproj_kernel, alpha=alpha),
        out_shape=(SDS((m, d), F32), SDS((m, ROUTE_W), F32)),
        grid=(m // tm,),
        in_specs=[pl.BlockSpec((tm, d), lambda i: (i, 0)),
                  pl.BlockSpec((None, d, d), lambda i: (l, 0, 0), **const),
                  pl.BlockSpec((tm, d), lambda i: (i, 0)),
                  pl.BlockSpec((None, 1, d), lambda i: (l, 0, 0)),
                  pl.BlockSpec((None, 1, d), lambda i: (l, 0, 0)),
                  pl.BlockSpec((None, d, ROUTE_W), lambda i: (0, 0, 0)),
                  pl.BlockSpec((None, 1, ROUTE_W), lambda i: (0, 0, 0))],
        out_specs=(pl.BlockSpec((tm, d), lambda i: (i, 0)),
                   pl.BlockSpec((tm, ROUTE_W), lambda i: (i, 0))),
        compiler_params=_cparams(1),
        name="outproj_ln_route",
    )(y, w_out, x, ln_g, ln_b, w_r, b_r)


def _row_copy(src_hbm, row, dst, r, sem):
    src = src_hbm.at[pl.ds(pl.multiple_of(row * ROW_TILES, ROW_TILES), ROW_TILES), :]
    return pltpu.make_async_copy(src, dst.at[pl.ds(pl.multiple_of(r * ROW_TILES, ROW_TILES), ROW_TILES), :], sem)


def _tiled_to_wide(ref, n):
    return jnp.concatenate([ref[pl.ds(s, n, stride=ROW_TILES), :] for s in range(ROW_TILES)], axis=1)


def _wide_to_tiled(val, ref):
    n = val.shape[0]
    for s in range(ROW_TILES):
        ref[pl.ds(s, n, stride=ROW_TILES), :] = val[:, s * LANES:(s + 1) * LANES]


def _gather_rows(src_hbm, idx_ref, base, dst, sem, n):
    def issue(r, c):
        _row_copy(src_hbm, idx_ref[base + r], dst, r, sem).start(priority=GATHER_DMA_PRIORITY)
        return c

    lax.fori_loop(0, n, issue, 0, unroll=GATHER_UNROLL)


def _gather_wait(src_hbm, dst, sem, n):
    def wait(r, c):
        _row_copy(src_hbm, 0, dst, r, sem).wait()
        return c

    lax.fori_loop(0, n, wait, 0, unroll=GATHER_UNROLL)


def _moe_kernel(bs_ref, nb_ref, nu_ref, tok_ref, x_hbm, wg_ref, wu_ref, wd_ref, y_hbm,
                xbuf, obuf, gsem, osem, *, bm, n_blocks):
    e = pl.program_id(0)
    n_used = nu_ref[0]

    def out_copy(g, slot):
        rows = pl.ds(pl.multiple_of(g * (bm * ROW_TILES), bm * ROW_TILES), bm * ROW_TILES)
        return pltpu.make_async_copy(obuf.at[slot], y_hbm.at[rows, :], osem.at[slot])

    @pl.when(e == 0)
    def _():
        _gather_rows(x_hbm, tok_ref, 0, xbuf.at[0], gsem.at[0], bm)

    def body(j, c):
        g = bs_ref[e] + j
        slot = lax.rem(g, 2)

        @pl.when(g + 1 < n_used)
        def _():
            _gather_rows(x_hbm, tok_ref, (g + 1) * bm, xbuf.at[1 - slot], gsem.at[1 - slot], bm)

        _gather_wait(x_hbm, xbuf.at[slot], gsem.at[slot], bm)

        @pl.when(g >= 2)
        def _():
            out_copy(g - 2, slot).wait()

        xb = _tiled_to_wide(xbuf.at[slot], bm).astype(BF16)
        gt = _dot_nt(xb, wg_ref[...].astype(BF16))
        up = _dot_nt(xb, wu_ref[...].astype(BF16))
        hid = (gt * _sigmoid(gt)) * up
        _wide_to_tiled(jnp.dot(hid.astype(BF16), wd_ref[...].astype(BF16), preferred_element_type=F32),
                       obuf.at[slot])
        out_copy(g, slot).start()
        return c

    lax.fori_loop(0, nb_ref[e], body, 0)

    @pl.when(e == pl.num_programs(0) - 1)
    def _():
        @pl.when(n_used >= 2)
        def _():
            out_copy(n_used - 2, lax.rem(n_used - 2, 2)).wait()

        out_copy(n_used - 1, lax.rem(n_used - 1, 2)).wait()

        obuf[0] = jnp.zeros(obuf.shape[1:], obuf.dtype)

        def fill(g, c):
            cp = out_copy(g, 0)
            cp.start()
            cp.wait()
            return c

        lax.fori_loop(n_used, n_blocks, fill, 0)


def _moe_ffn(x_tiled, bstart, nblk, n_used, row_tok, wgt, wut, wd, l, bm):
    f, d = wd.shape[2], wd.shape[3]
    assert d == ROW_TILES * LANES
    n_blocks = row_tok.shape[0] // bm
    w_spec = pl.BlockSpec((None, None, f, d), lambda e, bs, nb, nu, tok: (l, e, 0, 0))
    grid_spec = pltpu.PrefetchScalarGridSpec(
        num_scalar_prefetch=4,
        grid=(N_EXPERTS,),
        in_specs=[pl.BlockSpec(memory_space=pl.ANY), w_spec, w_spec, w_spec],
        out_specs=pl.BlockSpec(memory_space=pl.ANY),
        scratch_shapes=[pltpu.VMEM((2, bm * ROW_TILES, LANES), F32),
                        pltpu.VMEM((2, bm * ROW_TILES, LANES), F32),
                        pltpu.SemaphoreType.DMA((2,)), pltpu.SemaphoreType.DMA((2,))],
    )
    return pl.pallas_call(
        functools.partial(_moe_kernel, bm=bm, n_blocks=n_blocks),
        out_shape=SDS((n_blocks * bm * ROW_TILES, LANES), F32),
        grid_spec=grid_spec,
        compiler_params=_cparams(1),
        name="moe_ffn",
    )(bstart, nblk, n_used, row_tok, x_tiled, wgt, wut, wd)


def _combine_kernel(pos_ref, yb_hbm, x_ref, route_ref, g_ref, b_ref, xo_ref, xbo_ref,
                    buf, sems, *, tm, t_total, alpha):
    i = pl.program_id(0)
    slot = lax.rem(i, 2)

    def gather(step, s):
        for k in range(TOP_K):
            _gather_rows(yb_hbm, pos_ref, k * t_total + step * tm, buf.at[s, k],
                         sems.at[s * TOP_K + k], tm)

    @pl.when(i == 0)
    def _():
        gather(0, 0)

    @pl.when(i + 1 < pl.num_programs(0))
    def _():
        gather(i + 1, 1 - slot)

    for k in range(TOP_K):
        _gather_wait(yb_hbm, buf.at[slot, k], sems.at[slot * TOP_K + k], tm)

    rec = route_ref[...]
    z = (alpha * x_ref[...] + rec[:, 2:3] * _tiled_to_wide(buf.at[slot, 0], tm)
         + rec[:, 3:4] * _tiled_to_wide(buf.at[slot, 1], tm))
    xn = _layer_norm(z, g_ref[...], b_ref[...])
    xo_ref[...] = xn
    xbo_ref[...] = xn.astype(BF16)


def _combine_ln(pos_k, yb, x, route, ln_g, ln_b, l, alpha, tm):
    t, d = x.shape
    tm = min(tm, t)
    grid_spec = pltpu.PrefetchScalarGridSpec(
        num_scalar_prefetch=1,
        grid=(t // tm,),
        in_specs=[pl.BlockSpec(memory_space=pl.ANY),
                  pl.BlockSpec((tm, d), lambda i, pos: (i, 0)),
                  pl.BlockSpec((tm, ROUTE_W), lambda i, pos: (i, 0)),
                  pl.BlockSpec((None, 1, d), lambda i, pos: (l, 0, 0)),
                  pl.BlockSpec((None, 1, d), lambda i, pos: (l, 0, 0))],
        out_specs=(pl.BlockSpec((tm, d), lambda i, pos: (i, 0)),
                   pl.BlockSpec((tm, d), lambda i, pos: (i, 0))),
        scratch_shapes=[pltpu.VMEM((2, TOP_K, tm * ROW_TILES, LANES), F32),
                        pltpu.SemaphoreType.DMA((2 * TOP_K,))],
    )
    return pl.pallas_call(
        functools.partial(_combine_kernel, tm=tm, t_total=t, alpha=alpha),
        out_shape=(SDS((t, d), F32), SDS((t, d), BF16)),
        grid_spec=grid_spec,
        compiler_params=_cparams(1),
        name="combine_ln",
    )(pos_k, yb, x, route, ln_g, ln_b)


def _route_meta(e_ids, bm):
    t = e_ids.shape[0]
    a = t * TOP_K
    flat_e = e_ids.reshape(a)
    onehot = (flat_e[:, None] == jnp.arange(N_EXPERTS, dtype=I32)[None, :]).astype(I32)
    csum = jnp.cumsum(onehot, axis=0)
    rank = jnp.take_along_axis(csum, flat_e[:, None], axis=1)[:, 0] - 1
    counts = csum[-1]
    nblk = (counts + bm - 1) // bm
    bend = jnp.cumsum(nblk)
    bstart = bend - nblk
    dest = (bstart[flat_e] * bm + rank).astype(I32)
    n_blocks = a // bm + N_EXPERTS
    row_tok = jnp.zeros((n_blocks * bm,), I32).at[dest].set(jnp.arange(a, dtype=I32) // TOP_K)
    pos_k = dest.reshape(t, TOP_K).T.reshape(a)
    return pos_k, row_tok, bstart.astype(I32), nblk.astype(I32), bend[-1:].astype(I32)


def _swap_halves(w, axis):
    half = w.shape[axis] // 2
    lo = lax.slice_in_dim(w, 0, half, axis=axis)
    hi = lax.slice_in_dim(w, half, 2 * half, axis=axis)
    return jnp.concatenate([-hi, lo], axis=axis)


def _prep_layer_weights(wk_t, w_uq_l, w_ukv_l, w_rg_l, b_rg_l, w_re_l, b_re_l):
    d = wk_t.shape[1]
    w_krt = jnp.concatenate([wk_t, _swap_halves(wk_t, 0)], axis=0)
    wq = w_uq_l.reshape(MLA_Q_RANK, MLA_HEADS, MLA_NOPE_DIM + MLA_ROPE_DIM)
    wq_rope = wq[..., MLA_NOPE_DIM:]
    w_uq2 = jnp.concatenate([wq[..., :MLA_NOPE_DIM], wq_rope, _swap_halves(wq_rope, 2)],
                            axis=-1).reshape(MLA_Q_RANK, MLA_HEADS * MLA_QCAT)
    wkv = w_ukv_l.reshape(MLA_KV_RANK, MLA_HEADS, MLA_NOPE_DIM + MLA_V_DIM)
    w_ukv2 = jnp.concatenate([wkv[..., :MLA_NOPE_DIM].reshape(MLA_KV_RANK, -1),
                              wkv[..., MLA_NOPE_DIM:].reshape(MLA_KV_RANK, -1)], axis=1)
    pad = ROUTE_W - N_GROUPS - N_EXPERTS
    w_r = jnp.concatenate([w_rg_l, w_re_l, jnp.zeros((d, pad), F32)], axis=1)[None]
    b_r = jnp.concatenate([b_rg_l, b_re_l, jnp.zeros((pad,), F32)])[None, None]
    return w_krt, w_uq2, w_ukv2, w_r, b_r


def _rope_tables(seq):
    inv = ROPE_THETA ** (-jnp.arange(0, MLA_ROPE_DIM, 2, dtype=F32) / MLA_ROPE_DIM)
    ang = jnp.arange(seq, dtype=F32)[:, None] * inv[None, :]
    return jnp.cos(ang), jnp.sin(ang)


def kernel(x, w_in, da_lambda, da_subln_g, mla_q_norm_g, mla_w_uq, mla_kv_norm_g, mla_w_ukv, w_branch_a, w_branch_b, w_out, rel_bias, ln1_g, ln1_b, router_w_group, router_b_group, router_w_expert, router_b_expert, expert_w_gate, expert_w_up, expert_w_down, ln2_g, ln2_b):
    batch, seq, d = x.shape
    depth = w_in.shape[0]
    t = batch * seq
    alpha = (2 * depth) ** 0.25
    main_w = DA_QK_W * 2 + DA_V_W + MLA_Q_RANK + MLA_KV_RANK
    gate0 = main_w + MLA_ROPE_DIM

    cos, sin = _rope_tables(seq)
    tabk = jnp.concatenate([cos, cos, sin, sin], axis=1)
    q_scale = (MLA_NOPE_DIM + MLA_ROPE_DIM) ** -0.5
    tabq = q_scale * jnp.concatenate([jnp.ones((seq, MLA_NOPE_DIM), F32), tabk], axis=1)

    subln_g = da_subln_g[:, None, :]
    ln1_g3, ln1_b3 = ln1_g[:, None, :], ln1_b[:, None, :]
    ln2_g3, ln2_b3 = ln2_g[:, None, :], ln2_b[:, None, :]

    wt_in = jnp.swapaxes(w_in, 1, 2)
    wgt = jnp.swapaxes(expert_w_gate, 2, 3)
    wut = jnp.swapaxes(expert_w_up, 2, 3)

    xf = x.reshape(t, d)
    xb = xf.astype(BF16)
    for l in range(depth):
        lam_init = 0.8 - 0.6 * math.exp(-0.3 * l)
        w_krt, w_uq2, w_ukv2, w_r, b_r = _prep_layer_weights(
            wt_in[l, main_w:gate0, :], mla_w_uq[l], mla_w_ukv[l], router_w_group[l],
            router_b_group[l], router_w_expert[l], router_b_expert[l])

        h_main = _matmul_nt(xb, wt_in, l, 0, main_w, 2048, 256, BF16, "mm_main")
        h_g = _matmul_nt(xb, wt_in, l, gate0, 2 * d, 1024, 512, BF16, "mm_gates")
        kr2 = _krope(xb, w_krt, tabk, seq, 512)
        o_a = _da_attention(h_main, rel_bias, da_lambda, subln_g, l, lam_init, batch, seq)
        q_cat = _uq(h_main, mla_q_norm_g[l][None], w_uq2, tabq, seq, 512)
        kv = _ukv(h_main, mla_kv_norm_g[l][None], w_ukv2, 512)
        o_b = _mla_attention(q_cat, kv, kr2, batch, seq)
        y = _gated(o_a, o_b, w_branch_a, w_branch_b, h_g, l, 1024, 512)
        x1, route = _outproj_ln_route(y, w_out, xf, ln1_g3, ln1_b3, w_r, b_r, l, alpha, 256)

        e_ids = route[:, :TOP_K].astype(I32)
        pos_k, row_tok, bstart, nblk, n_used = _route_meta(e_ids, MOE_BLOCK)
        x1_tiled = x1.reshape(t * ROW_TILES, LANES)
        yb = _moe_ffn(x1_tiled, bstart, nblk, n_used, row_tok, wgt, wut, expert_w_down, l, MOE_BLOCK)
        xf, xb = _combine_ln(pos_k, yb, x1, route, ln2_g3, ln2_b3, l, alpha, 256)
    return xf.reshape(batch, seq, d)
```

```python
import functools
import math

import jax
import jax.numpy as jnp
from jax import lax
from jax.experimental import pallas as pl
from jax.experimental.pallas import tpu as pltpu

F32 = jnp.float32
BF16 = jnp.bfloat16
I32 = jnp.int32
SDS = jax.ShapeDtypeStruct

DA_HEADS = 8
DA_HEAD_DIM = 64
DA_V_DIM = 2 * DA_HEAD_DIM
MLA_HEADS = 8
MLA_Q_RANK = 768
MLA_KV_RANK = 512
MLA_NOPE_DIM = 128
MLA_ROPE_DIM = 64
MLA_V_DIM = 128
ROPE_THETA = 10000.0
REL_BUCKETS = 32
REL_MAX_DIST = 128
REL_MAX_EXACT = REL_BUCKETS // 2
N_GROUPS = 4
EXPERTS_PER_GROUP = 8
N_EXPERTS = N_GROUPS * EXPERTS_PER_GROUP
TOP_K = 2
NORM_EPS = 1e-5
NEG_INF = -1e30

DA_QK_W = DA_HEADS * 2 * DA_HEAD_DIM
DA_V_W = DA_HEADS * DA_V_DIM
MLA_O_W = MLA_HEADS * MLA_V_DIM
MLA_QCAT = 2 * MLA_NOPE_DIM

LANES = 128
SUBLANES = 8
VMEM_LIMIT = 56 * 1024 * 1024

ATT_K_BLOCK = 512
DA_Q_BLOCK = 256
MOE_BLOCK = 256
ROUTE_W = LANES
GATHER_UNROLL = 8
ROW_TILES = 16
HEADS_PER_STEP = 4
DMA_QUEUES = 2


def _cparams(n_axes):
    return pltpu.CompilerParams(dimension_semantics=("arbitrary",) * n_axes,
                                vmem_limit_bytes=VMEM_LIMIT)


def _dot_nt(a, b):
    return lax.dot_general(a, b, (((1,), (1,)), ((), ())), preferred_element_type=F32)


def _mm_nt_kernel(a_ref, w_ref, o_ref):
    o_ref[...] = _dot_nt(a_ref[...], w_ref[0].astype(BF16)).astype(o_ref.dtype)


def _matmul_nt(a, wt3, l, row0, n_rows, tm, tn, out_dtype, name):
    m, k = a.shape
    tm = min(tm, m)
    assert m % tm == 0 and n_rows % tn == 0
    if row0 % tn == 0:
        r0 = row0 // tn
        w_spec = pl.BlockSpec((1, tn, k), lambda i, j: (l, r0 + j, 0))
    else:
        w_spec = pl.BlockSpec((pl.Element(1), pl.Element(tn), pl.Element(k)),
                              lambda i, j: (l, pl.multiple_of(row0 + j * tn, SUBLANES), 0))
    return pl.pallas_call(
        _mm_nt_kernel,
        out_shape=SDS((m, n_rows), out_dtype),
        grid=(m // tm, n_rows // tn),
        in_specs=[pl.BlockSpec((tm, k), lambda i, j: (i, 0)), w_spec],
        out_specs=pl.BlockSpec((tm, tn), lambda i, j: (i, j)),
        compiler_params=_cparams(2),
        name=name,
    )(a, wt3)


def _krope_kernel(a_ref, w_ref, tab_ref, o_ref):
    t = _dot_nt(a_ref[...], w_ref[...].astype(BF16))
    t = t * tab_ref[...]
    o_ref[...] = (t + pltpu.roll(t, MLA_ROPE_DIM, axis=1)).astype(o_ref.dtype)


def _krope(xb, w_kr, tabk, seq, tm):
    m, k = xb.shape
    tm = min(tm, seq)
    nsb = seq // tm
    return pl.pallas_call(
        _krope_kernel,
        out_shape=SDS((m, LANES), BF16),
        grid=(m // tm,),
        in_specs=[pl.BlockSpec((tm, k), lambda i: (i, 0)),
                  pl.BlockSpec((LANES, k), lambda i: (0, 0)),
                  pl.BlockSpec((tm, LANES), lambda i: (i % nsb, 0))],
        out_specs=pl.BlockSpec((tm, LANES), lambda i: (i, 0)),
        compiler_params=_cparams(1),
        name="krope",
    )(xb, w_kr, tabk)


def _rms(c, g):
    return c * lax.rsqrt(jnp.mean(c * c, axis=-1, keepdims=True) + NORM_EPS) * g


def _uq_kernel(c_ref, g_ref, w_ref, tab_ref, o_ref):
    n = _rms(c_ref[...].astype(F32), g_ref[...])
    acc = jnp.dot(n.astype(BF16), w_ref[...].astype(BF16), preferred_element_type=F32)
    tab = tab_ref[...]
    for h in range(MLA_HEADS):
        sl = slice(h * MLA_QCAT, (h + 1) * MLA_QCAT)
        o_ref[:, sl] = (acc[:, sl] * tab).astype(o_ref.dtype)


def _uq(h_main, g, w_uq2, tabq, seq, tm):
    m = h_main.shape[0]
    tm = min(tm, seq)
    nsb = seq // tm
    cq_blk = (DA_QK_W * 2 + DA_V_W) // MLA_Q_RANK
    n_out = MLA_HEADS * MLA_QCAT
    return pl.pallas_call(
        _uq_kernel,
        out_shape=SDS((m, n_out), BF16),
        grid=(m // tm,),
        in_specs=[pl.BlockSpec((tm, MLA_Q_RANK), lambda i: (i, cq_blk)),
                  pl.BlockSpec((1, MLA_Q_RANK), lambda i: (0, 0)),
                  pl.BlockSpec((MLA_Q_RANK, n_out), lambda i: (0, 0)),
                  pl.BlockSpec((tm, MLA_QCAT), lambda i: (i % nsb, 0))],
        out_specs=pl.BlockSpec((tm, n_out), lambda i: (i, 0)),
        compiler_params=_cparams(1),
        name="mla_uq",
    )(h_main, g, w_uq2, tabq)


def _ukv_kernel(c0_ref, c1_ref, g_ref, w_ref, o_ref):
    c = jnp.concatenate([c0_ref[...], c1_ref[...]], axis=1).astype(F32)
    n = _rms(c, g_ref[...])
    o_ref[...] = jnp.dot(n.astype(BF16), w_ref[...].astype(BF16),
                         preferred_element_type=F32).astype(o_ref.dtype)


def _ukv(h_main, g, w_ukv2, tm):
    m = h_main.shape[0]
    tm = min(tm, m)
    half = MLA_KV_RANK // 2
    b0 = (DA_QK_W * 2 + DA_V_W + MLA_Q_RANK) // half
    n_out = w_ukv2.shape[1]
    return pl.pallas_call(
        _ukv_kernel,
        out_shape=SDS((m, n_out), BF16),
        grid=(m // tm,),
        in_specs=[pl.BlockSpec((tm, half), lambda i: (i, b0)),
                  pl.BlockSpec((tm, half), lambda i: (i, b0 + 1)),
                  pl.BlockSpec((1, MLA_KV_RANK), lambda i: (0, 0)),
                  pl.BlockSpec((MLA_KV_RANK, n_out), lambda i: (0, 0))],
        out_specs=pl.BlockSpec((tm, n_out), lambda i: (i, 0)),
        compiler_params=_cparams(1),
        name="mla_ukv",
    )(h_main, h_main, g, w_ukv2)


def _flash_init(m_scr, acc_scr):
    m_scr[...] = jnp.full(m_scr.shape, NEG_INF, F32)
    acc_scr[...] = jnp.zeros(acc_scr.shape, F32)


def _flash_update(s, v, m_scr, acc_scr):
    v1 = jnp.concatenate([v, jnp.ones(v.shape, v.dtype)], axis=1)
    m_prev = m_scr[...]
    m_new = jnp.maximum(m_prev, jnp.max(s, axis=1, keepdims=True))
    p = jnp.exp(s - jnp.concatenate([m_new] * (s.shape[1] // LANES), axis=1))
    alpha = jnp.exp(m_prev - m_new)
    acc_scr[...] = (jnp.concatenate([alpha, alpha], axis=1) * acc_scr[...]
                    + jnp.dot(p.astype(BF16), v1, preferred_element_type=F32))
    m_scr[...] = m_new


def _flash_result(acc_scr):
    acc = acc_scr[...]
    return acc[:, :LANES] / acc[:, LANES:]


def _t5_bias_blocks(rb_ref, h, bias_scr, tq, tk):
    row = lax.broadcasted_iota(I32, (tq, tk), 0)
    col = lax.broadcasted_iota(I32, (tq, tk), 1)
    far = rb_ref[REL_BUCKETS - 1, h]
    for d in range(bias_scr.shape[0]):
        n = row - col + d * tq
        nn = jnp.maximum(n, 0)
        nf = jnp.maximum(nn, 1).astype(F32)
        large = REL_MAX_EXACT + (jnp.log(nf / REL_MAX_EXACT) / math.log(REL_MAX_DIST / REL_MAX_EXACT)
                                 * (REL_BUCKETS - REL_MAX_EXACT)).astype(I32)
        large = jnp.minimum(large, REL_BUCKETS - 1)
        bucket = jnp.where(nn < REL_MAX_EXACT, nn, large)
        val = jnp.zeros((tq, tk), F32)
        for bb in range(REL_BUCKETS):
            val = jnp.where(bucket == bb, rb_ref[bb, h], val)
        bias_scr[d] = jnp.where(n >= 0, val - far, NEG_INF)


def _da_kernel(rb_ref, lamv_ref, q_ref, k_ref, v_ref, g_ref, o_ref,
               bias_scr, m_scr, acc_scr, *, tq, tk, lam_init):
    hp = pl.program_id(0)
    b = pl.program_id(1)
    qi = pl.program_id(2)
    ratio = tk // tq
    jd = qi // ratio
    par = qi % ratio
    hd = 2 * DA_HEAD_DIM
    heads = range(HEADS_PER_STEP)

    @pl.when((b == 0) & (qi == 0))
    def _():
        for hh in heads:
            _t5_bias_blocks(rb_ref, hp * HEADS_PER_STEP + hh, bias_scr.at[hh], tq, tk)

    scale = DA_HEAD_DIM ** -0.5
    q_all = q_ref[...]
    q2 = []
    for hh in heads:
        q = q_all[:, hh * hd:(hh + 1) * hd]
        lane = lax.broadcasted_iota(I32, q.shape, 1)
        zero = jnp.zeros_like(q)
        q2.append(jnp.concatenate([jnp.where(lane < DA_HEAD_DIM, q, zero),
                                   jnp.where(lane >= DA_HEAD_DIM, q, zero)], axis=0) * scale)
        _flash_init(m_scr.at[hh], acc_scr.at[hh])

    def block(j, bias_idx):
        off = pl.multiple_of(j * tk, tk)
        kb = k_ref[pl.ds(off, tk), :]
        vb = v_ref[pl.ds(off, tk), :]
        for hh in heads:
            s = _dot_nt(q2[hh], kb[:, hh * hd:(hh + 1) * hd])
            if bias_idx is not None:
                bias = bias_scr[hh, bias_idx]
                s = s + jnp.concatenate([bias, bias], axis=0)
            _flash_update(s, vb[:, hh * DA_V_DIM:(hh + 1) * DA_V_DIM], m_scr.at[hh], acc_scr.at[hh])

    def far_body(j, c):
        block(j, None)
        return c

    lax.fori_loop(0, jd - 1, far_body, 0)

    @pl.when((jd >= 1) & (par == 0))
    def _():
        block(jd - 1, ratio)

    @pl.when((jd >= 1) & (par != 0))
    def _():
        block(jd - 1, None)

    block(jd, par)

    lamv = lamv_ref[...]
    lam = (jnp.exp(jnp.sum(lamv[0:1] * lamv[1:2], axis=1, keepdims=True))
           - jnp.exp(jnp.sum(lamv[2:3] * lamv[3:4], axis=1, keepdims=True)) + lam_init)
    for hh in heads:
        o12 = _flash_result(acc_scr.at[hh])
        o = o12[:tq] - lam * o12[tq:]
        o = o * lax.rsqrt(jnp.mean(o * o, axis=1, keepdims=True) + NORM_EPS) * g_ref[...] * (1.0 - lam_init)
        o_ref[:, hh * DA_V_DIM:(hh + 1) * DA_V_DIM] = o.astype(o_ref.dtype)


def _da_attention(h_main, rel_bias, da_lambda, subln_g, l, lam_init, batch, seq):
    tq, tk = DA_Q_BLOCK, ATT_K_BLOCK
    ratio = tk // tq
    assert seq % tk == 0 and tk % tq == 0 and tq + 1 >= REL_MAX_DIST
    nq = seq // tq
    hw = HEADS_PER_STEP * 2 * DA_HEAD_DIM
    vw = HEADS_PER_STEP * DA_V_DIM
    kcol = DA_QK_W // hw
    vcol = 2 * DA_QK_W // vw
    return pl.pallas_call(
        functools.partial(_da_kernel, tq=tq, tk=tk, lam_init=lam_init),
        out_shape=SDS((batch * seq, DA_V_W), BF16),
        grid=(DA_HEADS // HEADS_PER_STEP, batch, nq),
        in_specs=[pl.BlockSpec(memory_space=pltpu.SMEM),
                  pl.BlockSpec((None, 4, DA_HEAD_DIM), lambda h, b, qi: (l, 0, 0)),
                  pl.BlockSpec((tq, hw), lambda h, b, qi: (b * nq + qi, h)),
                  pl.BlockSpec((seq, hw), lambda h, b, qi: (b, kcol + h)),
                  pl.BlockSpec((seq, vw), lambda h, b, qi: (b, vcol + h)),
                  pl.BlockSpec((None, 1, DA_V_DIM), lambda h, b, qi: (l, 0, 0))],
        out_specs=pl.BlockSpec((tq, vw), lambda h, b, qi: (b * nq + qi, h)),
        scratch_shapes=[pltpu.VMEM((HEADS_PER_STEP, ratio + 1, tq, tk), F32),
                        pltpu.VMEM((HEADS_PER_STEP, 2 * tq, LANES), F32),
                        pltpu.VMEM((HEADS_PER_STEP, 2 * tq, 2 * LANES), F32)],
        compiler_params=_cparams(3),
        name="da_attn",
    )(rel_bias, da_lambda, h_main, h_main, h_main, subln_g)


def _mla_kernel(q_ref, kn_ref, kr_ref, v_ref, o_ref, m_scr, acc_scr, *, blk):
    qi = pl.program_id(2)
    heads = range(HEADS_PER_STEP)
    q_all = q_ref[...]
    q = [q_all[:, hh * MLA_QCAT:(hh + 1) * MLA_QCAT] for hh in heads]
    for hh in heads:
        _flash_init(m_scr.at[hh], acc_scr.at[hh])

    def block(j, masked):
        off = pl.multiple_of(j * blk, blk)
        kn = kn_ref[pl.ds(off, blk), :]
        kr = kr_ref[pl.ds(off, blk), :]
        vb = v_ref[pl.ds(off, blk), :]
        for hh in heads:
            kcat = jnp.concatenate([kn[:, hh * MLA_NOPE_DIM:(hh + 1) * MLA_NOPE_DIM], kr], axis=1)
            s = _dot_nt(q[hh], kcat)
            if masked:
                row = lax.broadcasted_iota(I32, s.shape, 0)
                col = lax.broadcasted_iota(I32, s.shape, 1)
                s = jnp.where(col <= row, s, NEG_INF)
            _flash_update(s, vb[:, hh * MLA_V_DIM:(hh + 1) * MLA_V_DIM], m_scr.at[hh], acc_scr.at[hh])

    def far_body(j, c):
        block(j, False)
        return c

    lax.fori_loop(0, qi, far_body, 0)
    block(qi, True)
    for hh in heads:
        o_ref[:, hh * MLA_V_DIM:(hh + 1) * MLA_V_DIM] = _flash_result(acc_scr.at[hh]).astype(o_ref.dtype)


def _mla_attention(q_cat, kv, kr2, batch, seq):
    blk = min(ATT_K_BLOCK, seq)
    nq = seq // blk
    hps = HEADS_PER_STEP
    vcol = MLA_HEADS // hps
    return pl.pallas_call(
        functools.partial(_mla_kernel, blk=blk),
        out_shape=SDS((batch * seq, MLA_O_W), BF16),
        grid=(batch, MLA_HEADS // hps, nq),
        in_specs=[pl.BlockSpec((blk, hps * MLA_QCAT), lambda b, h, qi: (b * nq + qi, h)),
                  pl.BlockSpec((seq, hps * MLA_NOPE_DIM), lambda b, h, qi: (b, h)),
                  pl.BlockSpec((seq, LANES), lambda b, h, qi: (b, 0)),
                  pl.BlockSpec((seq, hps * MLA_V_DIM), lambda b, h, qi: (b, vcol + h))],
        out_specs=pl.BlockSpec((blk, hps * MLA_V_DIM), lambda b, h, qi: (b * nq + qi, h)),
        scratch_shapes=[pltpu.VMEM((hps, blk, LANES), F32),
                        pltpu.VMEM((hps, blk, 2 * LANES), F32)],
        compiler_params=_cparams(3),
        name="mla_attn",
    )(q_cat, kv, kr2, kv)


def _sigmoid(x):
    return 1.0 / (1.0 + jnp.exp(-x))


def _gated_kernel(oa_ref, ob_ref, wa_ref, wb_ref, ga_ref, gb_ref, o_ref):
    ya = jnp.dot(oa_ref[...], wa_ref[...].astype(BF16), preferred_element_type=F32)
    yb = jnp.dot(ob_ref[...], wb_ref[...].astype(BF16), preferred_element_type=F32)
    y = _sigmoid(ga_ref[...].astype(F32)) * ya + _sigmoid(gb_ref[...].astype(F32)) * yb
    o_ref[...] = y.astype(o_ref.dtype)


def _gated(o_a, o_b, w_a, w_b, h_g, l, tm, tn):
    m = o_a.shape[0]
    d = w_a.shape[2]
    tm = min(tm, m)
    ngb = d // tn
    return pl.pallas_call(
        _gated_kernel,
        out_shape=SDS((m, d), BF16),
        grid=(m // tm, ngb),
        in_specs=[pl.BlockSpec((tm, DA_V_W), lambda i, j: (i, 0)),
                  pl.BlockSpec((tm, MLA_O_W), lambda i, j: (i, 0)),
                  pl.BlockSpec((None, DA_V_W, tn), lambda i, j: (l, 0, j)),
                  pl.BlockSpec((None, MLA_O_W, tn), lambda i, j: (l, 0, j)),
                  pl.BlockSpec((tm, tn), lambda i, j: (i, j)),
                  pl.BlockSpec((tm, tn), lambda i, j: (i, ngb + j))],
        out_specs=pl.BlockSpec((tm, tn), lambda i, j: (i, j)),
        compiler_params=_cparams(2),
        name="gated",
    )(o_a, o_b, w_a, w_b, h_g, h_g)


def _layer_norm(z, g, b):
    mu = jnp.mean(z, axis=-1, keepdims=True)
    zc = z - mu
    var = jnp.mean(zc * zc, axis=-1, keepdims=True)
    return zc * lax.rsqrt(var + NORM_EPS) * g + b


def _split_bf16(x):
    hi = x.astype(BF16)
    lo = (x - hi.astype(F32)).astype(BF16)
    return hi, lo


def _route_record(lg):
    lane = lax.broadcasted_iota(I32, lg.shape, 1)
    lane_f = lane.astype(F32)
    big = float(ROUTE_W)
    gmask = lane < N_GROUPS
    g_max = jnp.max(jnp.where(gmask, lg, NEG_INF), axis=1, keepdims=True)
    g_idx = jnp.min(jnp.where(gmask & (lg == g_max), lane_f, big), axis=1, keepdims=True)
    g_w = 1.0 / jnp.sum(jnp.where(gmask, jnp.exp(lg - g_max), 0.0), axis=1, keepdims=True)

    lo = N_GROUPS + EXPERTS_PER_GROUP * g_idx
    sel = (lane_f >= lo) & (lane_f < lo + EXPERTS_PER_GROUP)
    e_max = jnp.max(jnp.where(sel, lg, NEG_INF), axis=1, keepdims=True)
    pe = jnp.where(sel, jnp.exp(lg - e_max), 0.0)
    prob = pe / jnp.sum(pe, axis=1, keepdims=True)
    p1 = jnp.max(jnp.where(sel, prob, -1.0), axis=1, keepdims=True)
    i1 = jnp.min(jnp.where(sel & (prob == p1), lane_f, big), axis=1, keepdims=True)
    sel2 = sel & (lane_f != i1)
    p2 = jnp.max(jnp.where(sel2, prob, -1.0), axis=1, keepdims=True)
    i2 = jnp.min(jnp.where(sel2 & (prob == p2), lane_f, big), axis=1, keepdims=True)
    tot = p1 + p2
    w1 = g_w * (p1 / tot)
    w2 = g_w * (p2 / tot)
    rec = jnp.where(lane == 0, i1 - N_GROUPS,
          jnp.where(lane == 1, i2 - N_GROUPS,
          jnp.where(lane == 2, w1, jnp.where(lane == 3, w2, 0.0))))
    return rec


def _outproj_kernel(y_ref, w_ref, x_ref, g_ref, b_ref, wr_ref, br_ref,
                    xo_ref, xt_ref, route_ref, *, alpha):
    mix = jnp.dot(y_ref[...], w_ref[...].astype(BF16), preferred_element_type=F32)
    xn = _layer_norm(alpha * x_ref[...] + mix, g_ref[...], b_ref[...])
    xo_ref[...] = xn
    _wide_to_tiled(xn, xt_ref)
    xh, xl = _split_bf16(xn)
    wh, wl = _split_bf16(wr_ref[...])
    lg = (jnp.dot(xh, wh, preferred_element_type=F32) + jnp.dot(xl, wh, preferred_element_type=F32)
          + jnp.dot(xh, wl, preferred_element_type=F32) + br_ref[...])
    route_ref[...] = _route_record(lg)


def _outproj_ln_route(y, w_out, x, ln_g, ln_b, w_r, b_r, l, alpha, tm):
    m, d = x.shape
    tm = min(tm, m)
    const = dict(pipeline_mode=pl.Buffered(1))
    return pl.pallas_call(
        functools.partial(_outproj_kernel, alpha=alpha),
        out_shape=(SDS((m, d), F32), SDS((m * ROW_TILES, LANES), F32), SDS((m, ROUTE_W), F32)),
        grid=(m // tm,),
        in_specs=[pl.BlockSpec((tm, d), lambda i: (i, 0)),
                  pl.BlockSpec((None, d, d), lambda i: (l, 0, 0), **const),
                  pl.BlockSpec((tm, d), lambda i: (i, 0)),
                  pl.BlockSpec((None, 1, d), lambda i: (l, 0, 0)),
                  pl.BlockSpec((None, 1, d), lambda i: (l, 0, 0)),
                  pl.BlockSpec((None, d, ROUTE_W), lambda i: (0, 0, 0)),
                  pl.BlockSpec((None, 1, ROUTE_W), lambda i: (0, 0, 0))],
        out_specs=(pl.BlockSpec((tm, d), lambda i: (i, 0)),
                   pl.BlockSpec((tm * ROW_TILES, LANES), lambda i: (i, 0)),
                   pl.BlockSpec((tm, ROUTE_W), lambda i: (i, 0))),
        compiler_params=_cparams(1),
        name="outproj_ln_route",
    )(y, w_out, x, ln_g, ln_b, w_r, b_r)


def _row_copy(src_hbm, row, dst, r, sem):
    src = src_hbm.at[pl.ds(pl.multiple_of(row * ROW_TILES, ROW_TILES), ROW_TILES), :]
    return pltpu.make_async_copy(src, dst.at[pl.ds(pl.multiple_of(r * ROW_TILES, ROW_TILES), ROW_TILES), :], sem)


def _tiled_to_wide(ref, n):
    return jnp.concatenate([ref[pl.ds(s, n, stride=ROW_TILES), :] for s in range(ROW_TILES)], axis=1)


def _wide_to_tiled(val, ref):
    n = val.shape[0]
    for s in range(ROW_TILES):
        ref[pl.ds(s, n, stride=ROW_TILES), :] = val[:, s * LANES:(s + 1) * LANES]


def _gather_rows(src_hbm, idx_ref, base, dst, sem, n):
    def issue(i, c):
        for u in range(GATHER_UNROLL):
            r = i * GATHER_UNROLL + u
            _row_copy(src_hbm, idx_ref[base + r], dst, r, sem).start(priority=u % DMA_QUEUES)
        return c

    lax.fori_loop(0, n // GATHER_UNROLL, issue, 0)


def _gather_wait(src_hbm, dst, sem, n):
    def wait(r, c):
        _row_copy(src_hbm, 0, dst, r, sem).wait()
        return c

    lax.fori_loop(0, n, wait, 0, unroll=GATHER_UNROLL)


def _moe_kernel(bs_ref, nb_ref, nu_ref, tok_ref, x_hbm, wg_ref, wu_ref, wd_ref, y_hbm,
                xbuf, obuf, gsem, osem, *, bm, n_blocks):
    e = pl.program_id(0)
    n_used = nu_ref[0]

    def out_copy(g, slot):
        rows = pl.ds(pl.multiple_of(g * (bm * ROW_TILES), bm * ROW_TILES), bm * ROW_TILES)
        return pltpu.make_async_copy(obuf.at[slot], y_hbm.at[rows, :], osem.at[slot])

    @pl.when(e == 0)
    def _():
        _gather_rows(x_hbm, tok_ref, 0, xbuf.at[0], gsem.at[0], bm)

    def body(j, c):
        g = bs_ref[e] + j
        slot = lax.rem(g, 2)

        @pl.when(g + 1 < n_used)
        def _():
            _gather_rows(x_hbm, tok_ref, (g + 1) * bm, xbuf.at[1 - slot], gsem.at[1 - slot], bm)

        _gather_wait(x_hbm, xbuf.at[slot], gsem.at[slot], bm)

        @pl.when(g >= 2)
        def _():
            out_copy(g - 2, slot).wait()

        xb = _tiled_to_wide(xbuf.at[slot], bm).astype(BF16)
        gt = _dot_nt(xb, wg_ref[...].astype(BF16))
        up = _dot_nt(xb, wu_ref[...].astype(BF16))
        hid = (gt * _sigmoid(gt)) * up
        _wide_to_tiled(jnp.dot(hid.astype(BF16), wd_ref[...].astype(BF16), preferred_element_type=F32),
                       obuf.at[slot])
        out_copy(g, slot).start()
        return c

    lax.fori_loop(0, nb_ref[e], body, 0)

    @pl.when(e == pl.num_programs(0) - 1)
    def _():
        @pl.when(n_used >= 2)
        def _():
            out_copy(n_used - 2, lax.rem(n_used - 2, 2)).wait()

        out_copy(n_used - 1, lax.rem(n_used - 1, 2)).wait()

        obuf[0] = jnp.zeros(obuf.shape[1:], obuf.dtype)

        def fill(g, c):
            cp = out_copy(g, 0)
            cp.start()
            cp.wait()
            return c

        lax.fori_loop(n_used, n_blocks, fill, 0)


def _moe_ffn(x_tiled, bstart, nblk, n_used, row_tok, wgt, wut, wd, l, bm):
    f, d = wd.shape[2], wd.shape[3]
    assert d == ROW_TILES * LANES
    n_blocks = row_tok.shape[0] // bm
    w_spec = pl.BlockSpec((None, None, f, d), lambda e, bs, nb, nu, tok: (l, e, 0, 0))
    grid_spec = pltpu.PrefetchScalarGridSpec(
        num_scalar_prefetch=4,
        grid=(N_EXPERTS,),
        in_specs=[pl.BlockSpec(memory_space=pl.ANY), w_spec, w_spec, w_spec],
        out_specs=pl.BlockSpec(memory_space=pl.ANY),
        scratch_shapes=[pltpu.VMEM((2, bm * ROW_TILES, LANES), F32),
                        pltpu.VMEM((2, bm * ROW_TILES, LANES), F32),
                        pltpu.SemaphoreType.DMA((2,)), pltpu.SemaphoreType.DMA((2,))],
    )
    return pl.pallas_call(
        functools.partial(_moe_kernel, bm=bm, n_blocks=n_blocks),
        out_shape=SDS((n_blocks * bm * ROW_TILES, LANES), F32),
        grid_spec=grid_spec,
        compiler_params=_cparams(1),
        name="moe_ffn",
    )(bstart, nblk, n_used, row_tok, x_tiled, wgt, wut, wd)


def _combine_kernel(pos_ref, yb_hbm, x_ref, route_ref, g_ref, b_ref, xo_ref, xbo_ref,
                    buf, sems, *, tm, t_total, alpha):
    i = pl.program_id(0)
    slot = lax.rem(i, 2)

    def gather(step, s):
        for k in range(TOP_K):
            _gather_rows(yb_hbm, pos_ref, k * t_total + step * tm, buf.at[s, k],
                         sems.at[s * TOP_K + k], tm)

    @pl.when(i == 0)
    def _():
        gather(0, 0)

    @pl.when(i + 1 < pl.num_programs(0))
    def _():
        gather(i + 1, 1 - slot)

    for k in range(TOP_K):
        _gather_wait(yb_hbm, buf.at[slot, k], sems.at[slot * TOP_K + k], tm)

    rec = route_ref[...]
    z = (alpha * x_ref[...] + rec[:, 2:3] * _tiled_to_wide(buf.at[slot, 0], tm)
         + rec[:, 3:4] * _tiled_to_wide(buf.at[slot, 1], tm))
    xn = _layer_norm(z, g_ref[...], b_ref[...])
    xo_ref[...] = xn
    xbo_ref[...] = xn.astype(BF16)


def _combine_ln(pos_k, yb, x, route, ln_g, ln_b, l, alpha, tm):
    t, d = x.shape
    tm = min(tm, t)
    grid_spec = pltpu.PrefetchScalarGridSpec(
        num_scalar_prefetch=1,
        grid=(t // tm,),
        in_specs=[pl.BlockSpec(memory_space=pl.ANY),
                  pl.BlockSpec((tm, d), lambda i, pos: (i, 0)),
                  pl.BlockSpec((tm, ROUTE_W), lambda i, pos: (i, 0)),
                  pl.BlockSpec((None, 1, d), lambda i, pos: (l, 0, 0)),
                  pl.BlockSpec((None, 1, d), lambda i, pos: (l, 0, 0))],
        out_specs=(pl.BlockSpec((tm, d), lambda i, pos: (i, 0)),
                   pl.BlockSpec((tm, d), lambda i, pos: (i, 0))),
        scratch_shapes=[pltpu.VMEM((2, TOP_K, tm * ROW_TILES, LANES), F32),
                        pltpu.SemaphoreType.DMA((2 * TOP_K,))],
    )
    return pl.pallas_call(
        functools.partial(_combine_kernel, tm=tm, t_total=t, alpha=alpha),
        out_shape=(SDS((t, d), F32), SDS((t, d), BF16)),
        grid_spec=grid_spec,
        compiler_params=_cparams(1),
        name="combine_ln",
    )(pos_k, yb, x, route, ln_g, ln_b)


def _route_meta(e_ids, bm):
    t = e_ids.shape[0]
    a = t * TOP_K
    flat_e = e_ids.reshape(a)
    onehot = (flat_e[:, None] == jnp.arange(N_EXPERTS, dtype=I32)[None, :]).astype(I32)
    csum = jnp.cumsum(onehot, axis=0)
    rank = jnp.take_along_axis(csum, flat_e[:, None], axis=1)[:, 0] - 1
    counts = csum[-1]
    nblk = (counts + bm - 1) // bm
    bend = jnp.cumsum(nblk)
    bstart = bend - nblk
    dest = (bstart[flat_e] * bm + rank).astype(I32)
    n_blocks = a // bm + N_EXPERTS
    row_tok = jnp.zeros((n_blocks * bm,), I32).at[dest].set(jnp.arange(a, dtype=I32) // TOP_K)
    pos_k = dest.reshape(t, TOP_K).T.reshape(a)
    return pos_k, row_tok, bstart.astype(I32), nblk.astype(I32), bend[-1:].astype(I32)


def _swap_halves(w, axis):
    half = w.shape[axis] // 2
    lo = lax.slice_in_dim(w, 0, half, axis=axis)
    hi = lax.slice_in_dim(w, half, 2 * half, axis=axis)
    return jnp.concatenate([-hi, lo], axis=axis)


def _prep_layer_weights(wk_t, w_uq_l, w_ukv_l, w_rg_l, b_rg_l, w_re_l, b_re_l):
    d = wk_t.shape[1]
    w_krt = jnp.concatenate([wk_t, _swap_halves(wk_t, 0)], axis=0)
    wq = w_uq_l.reshape(MLA_Q_RANK, MLA_HEADS, MLA_NOPE_DIM + MLA_ROPE_DIM)
    wq_rope = wq[..., MLA_NOPE_DIM:]
    w_uq2 = jnp.concatenate([wq[..., :MLA_NOPE_DIM], wq_rope, _swap_halves(wq_rope, 2)],
                            axis=-1).reshape(MLA_Q_RANK, MLA_HEADS * MLA_QCAT)
    wkv = w_ukv_l.reshape(MLA_KV_RANK, MLA_HEADS, MLA_NOPE_DIM + MLA_V_DIM)
    w_ukv2 = jnp.concatenate([wkv[..., :MLA_NOPE_DIM].reshape(MLA_KV_RANK, -1),
                              wkv[..., MLA_NOPE_DIM:].reshape(MLA_KV_RANK, -1)], axis=1)
    pad = ROUTE_W - N_GROUPS - N_EXPERTS
    w_r = jnp.concatenate([w_rg_l, w_re_l, jnp.zeros((d, pad), F32)], axis=1)[None]
    b_r = jnp.concatenate([b_rg_l, b_re_l, jnp.zeros((pad,), F32)])[None, None]
    return w_krt, w_uq2, w_ukv2, w_r, b_r


def _rope_tables(seq):
    inv = ROPE_THETA ** (-jnp.arange(0, MLA_ROPE_DIM, 2, dtype=F32) / MLA_ROPE_DIM)
    ang = jnp.arange(seq, dtype=F32)[:, None] * inv[None, :]
    return jnp.cos(ang), jnp.sin(ang)


def kernel(x, w_in, da_lambda, da_subln_g, mla_q_norm_g, mla_w_uq, mla_kv_norm_g, mla_w_ukv, w_branch_a, w_branch_b, w_out, rel_bias, ln1_g, ln1_b, router_w_group, router_b_group, router_w_expert, router_b_expert, expert_w_gate, expert_w_up, expert_w_down, ln2_g, ln2_b):
    batch, seq, d = x.shape
    depth = w_in.shape[0]
    t = batch * seq
    alpha = (2 * depth) ** 0.25
    main_w = DA_QK_W * 2 + DA_V_W + MLA_Q_RANK + MLA_KV_RANK
    gate0 = main_w + MLA_ROPE_DIM

    cos, sin = _rope_tables(seq)
    tabk = jnp.concatenate([cos, cos, sin, sin], axis=1)
    q_scale = (MLA_NOPE_DIM + MLA_ROPE_DIM) ** -0.5
    tabq = q_scale * jnp.concatenate([jnp.ones((seq, MLA_NOPE_DIM), F32), tabk], axis=1)

    subln_g = da_subln_g[:, None, :]
    ln1_g3, ln1_b3 = ln1_g[:, None, :], ln1_b[:, None, :]
    ln2_g3, ln2_b3 = ln2_g[:, None, :], ln2_b[:, None, :]

    wt_in = jnp.swapaxes(w_in, 1, 2)
    wgt = jnp.swapaxes(expert_w_gate, 2, 3)
    wut = jnp.swapaxes(expert_w_up, 2, 3)

    xf = x.reshape(t, d)
    xb = xf.astype(BF16)
    for l in range(depth):
        lam_init = 0.8 - 0.6 * math.exp(-0.3 * l)
        w_krt, w_uq2, w_ukv2, w_r, b_r = _prep_layer_weights(
            wt_in[l, main_w:gate0, :], mla_w_uq[l], mla_w_ukv[l], router_w_group[l],
            router_b_group[l], router_w_expert[l], router_b_expert[l])

        h_main = _matmul_nt(xb, wt_in, l, 0, main_w, 2048, 256, BF16, "mm_main")
        h_g = _matmul_nt(xb, wt_in, l, gate0, 2 * d, 1024, 512, BF16, "mm_gates")
        kr2 = _krope(xb, w_krt, tabk, seq, 512)
        o_a = _da_attention(h_main, rel_bias, da_lambda, subln_g, l, lam_init, batch, seq)
        q_cat = _uq(h_main, mla_q_norm_g[l][None], w_uq2, tabq, seq, 512)
        kv = _ukv(h_main, mla_kv_norm_g[l][None], w_ukv2, 512)
        o_b = _mla_attention(q_cat, kv, kr2, batch, seq)
        y = _gated(o_a, o_b, w_branch_a, w_branch_b, h_g, l, 1024, 512)
        x1, x1_tiled, route = _outproj_ln_route(y, w_out, xf, ln1_g3, ln1_b3, w_r, b_r, l, alpha, 256)

        e_ids = route[:, :TOP_K].astype(I32)
        pos_k, row_tok, bstart, nblk, n_used = _route_meta(e_ids, MOE_BLOCK)
        yb = _moe_ffn(x1_tiled, bstart, nblk, n_used, row_tok, wgt, wut, expert_w_down, l, MOE_BLOCK)
        xf, xb = _combine_ln(pos_k, yb, x1, route, ln2_g3, ln2_b3, l, alpha, 256)
    return xf.reshape(batch, seq, d)
```

```python
import functools
import math

import jax
import jax.numpy as jnp
from jax import lax
from jax.experimental import pallas as pl
from jax.experimental.pallas import tpu as pltpu

F32 = jnp.float32
BF16 = jnp.bfloat16
I32 = jnp.int32
SDS = jax.ShapeDtypeStruct

DA_HEADS = 8
DA_HEAD_DIM = 64
DA_V_DIM = 2 * DA_HEAD_DIM
MLA_HEADS = 8
MLA_Q_RANK = 768
MLA_KV_RANK = 512
MLA_NOPE_DIM = 128
MLA_ROPE_DIM = 64
MLA_V_DIM = 128
ROPE_THETA = 10000.0
REL_BUCKETS = 32
REL_MAX_DIST = 128
REL_MAX_EXACT = REL_BUCKETS // 2
N_GROUPS = 4
EXPERTS_PER_GROUP = 8
N_EXPERTS = N_GROUPS * EXPERTS_PER_GROUP
TOP_K = 2
NORM_EPS = 1e-5
NEG_INF = -1e30

DA_QK_W = DA_HEADS * 2 * DA_HEAD_DIM
DA_V_W = DA_HEADS * DA_V_DIM
MLA_O_W = MLA_HEADS * MLA_V_DIM
MLA_QCAT = 2 * MLA_NOPE_DIM

LANES = 128
SUBLANES = 8
VMEM_LIMIT = 56 * 1024 * 1024

ATT_K_BLOCK = 512
DA_Q_BLOCK = 256
MOE_BLOCK = 256
ROUTE_W = LANES
GATHER_UNROLL = 8
ROW_TILES = 16
HEADS_PER_STEP = 4
DMA_QUEUES = 2


def _cparams(n_axes):
    return pltpu.CompilerParams(dimension_semantics=("arbitrary",) * n_axes,
                                vmem_limit_bytes=VMEM_LIMIT)


def _dot_nt(a, b):
    return lax.dot_general(a, b, (((1,), (1,)), ((), ())), preferred_element_type=F32)


def _mm_nt_kernel(a_ref, w_ref, o_ref):
    o_ref[...] = _dot_nt(a_ref[...], w_ref[0].astype(BF16)).astype(o_ref.dtype)


def _matmul_nt(a, wt3, l, row0, n_rows, tm, tn, out_dtype, name):
    m, k = a.shape
    tm = min(tm, m)
    assert m % tm == 0 and n_rows % tn == 0
    if row0 % tn == 0:
        r0 = row0 // tn
        w_spec = pl.BlockSpec((1, tn, k), lambda i, j: (l, r0 + j, 0))
    else:
        w_spec = pl.BlockSpec((pl.Element(1), pl.Element(tn), pl.Element(k)),
                              lambda i, j: (l, pl.multiple_of(row0 + j * tn, SUBLANES), 0))
    return pl.pallas_call(
        _mm_nt_kernel,
        out_shape=SDS((m, n_rows), out_dtype),
        grid=(m // tm, n_rows // tn),
        in_specs=[pl.BlockSpec((tm, k), lambda i, j: (i, 0)), w_spec],
        out_specs=pl.BlockSpec((tm, tn), lambda i, j: (i, j)),
        compiler_params=_cparams(2),
        name=name,
    )(a, wt3)


def _krope_kernel(a_ref, w_ref, tab_ref, o_ref):
    t = _dot_nt(a_ref[...], w_ref[...].astype(BF16))
    t = t * tab_ref[...]
    o_ref[...] = (t + pltpu.roll(t, MLA_ROPE_DIM, axis=1)).astype(o_ref.dtype)


def _krope(xb, w_kr, tabk, seq, tm):
    m, k = xb.shape
    tm = min(tm, seq)
    nsb = seq // tm
    return pl.pallas_call(
        _krope_kernel,
        out_shape=SDS((m, LANES), BF16),
        grid=(m // tm,),
        in_specs=[pl.BlockSpec((tm, k), lambda i: (i, 0)),
                  pl.BlockSpec((LANES, k), lambda i: (0, 0)),
                  pl.BlockSpec((tm, LANES), lambda i: (i % nsb, 0))],
        out_specs=pl.BlockSpec((tm, LANES), lambda i: (i, 0)),
        compiler_params=_cparams(1),
        name="krope",
    )(xb, w_kr, tabk)


def _rms(c, g):
    return c * lax.rsqrt(jnp.mean(c * c, axis=-1, keepdims=True) + NORM_EPS) * g


def _uq_kernel(c_ref, g_ref, w_ref, tab_ref, o_ref):
    n = _rms(c_ref[...].astype(F32), g_ref[...])
    acc = jnp.dot(n.astype(BF16), w_ref[...].astype(BF16), preferred_element_type=F32)
    tab = tab_ref[...]
    for h in range(MLA_HEADS):
        sl = slice(h * MLA_QCAT, (h + 1) * MLA_QCAT)
        o_ref[:, sl] = (acc[:, sl] * tab).astype(o_ref.dtype)


def _uq(h_main, g, w_uq2, tabq, seq, tm):
    m = h_main.shape[0]
    tm = min(tm, seq)
    nsb = seq // tm
    cq_blk = (DA_QK_W * 2 + DA_V_W) // MLA_Q_RANK
    n_out = MLA_HEADS * MLA_QCAT
    return pl.pallas_call(
        _uq_kernel,
        out_shape=SDS((m, n_out), BF16),
        grid=(m // tm,),
        in_specs=[pl.BlockSpec((tm, MLA_Q_RANK), lambda i: (i, cq_blk)),
                  pl.BlockSpec((1, MLA_Q_RANK), lambda i: (0, 0)),
                  pl.BlockSpec((MLA_Q_RANK, n_out), lambda i: (0, 0)),
                  pl.BlockSpec((tm, MLA_QCAT), lambda i: (i % nsb, 0))],
        out_specs=pl.BlockSpec((tm, n_out), lambda i: (i, 0)),
        compiler_params=_cparams(1),
        name="mla_uq",
    )(h_main, g, w_uq2, tabq)


def _ukv_kernel(c0_ref, c1_ref, g_ref, w_ref, o_ref):
    c = jnp.concatenate([c0_ref[...], c1_ref[...]], axis=1).astype(F32)
    n = _rms(c, g_ref[...])
    o_ref[...] = jnp.dot(n.astype(BF16), w_ref[...].astype(BF16),
                         preferred_element_type=F32).astype(o_ref.dtype)


def _ukv(h_main, g, w_ukv2, tm):
    m = h_main.shape[0]
    tm = min(tm, m)
    half = MLA_KV_RANK // 2
    b0 = (DA_QK_W * 2 + DA_V_W + MLA_Q_RANK) // half
    n_out = w_ukv2.shape[1]
    return pl.pallas_call(
        _ukv_kernel,
        out_shape=SDS((m, n_out), BF16),
        grid=(m // tm,),
        in_specs=[pl.BlockSpec((tm, half), lambda i: (i, b0)),
                  pl.BlockSpec((tm, half), lambda i: (i, b0 + 1)),
                  pl.BlockSpec((1, MLA_KV_RANK), lambda i: (0, 0)),
                  pl.BlockSpec((MLA_KV_RANK, n_out), lambda i: (0, 0))],
        out_specs=pl.BlockSpec((tm, n_out), lambda i: (i, 0)),
        compiler_params=_cparams(1),
        name="mla_ukv",
    )(h_main, h_main, g, w_ukv2)


def _flash_init(m_scr, acc_scr):
    m_scr[...] = jnp.full(m_scr.shape, NEG_INF, F32)
    acc_scr[...] = jnp.zeros(acc_scr.shape, F32)


def _flash_update(s, v, m_scr, acc_scr):
    v1 = jnp.concatenate([v, jnp.ones(v.shape, v.dtype)], axis=1)
    m_prev = m_scr[...]
    m_new = jnp.maximum(m_prev, jnp.max(s, axis=1, keepdims=True))
    p = jnp.exp(s - jnp.concatenate([m_new] * (s.shape[1] // LANES), axis=1))
    alpha = jnp.exp(m_prev - m_new)
    acc_scr[...] = (jnp.concatenate([alpha, alpha], axis=1) * acc_scr[...]
                    + jnp.dot(p.astype(BF16), v1, preferred_element_type=F32))
    m_scr[...] = m_new


def _flash_result(acc_scr):
    acc = acc_scr[...]
    return acc[:, :LANES] / acc[:, LANES:]


def _t5_bias_blocks(rb_ref, h, bias_scr, tq, tk):
    row = lax.broadcasted_iota(I32, (tq, tk), 0)
    col = lax.broadcasted_iota(I32, (tq, tk), 1)
    far = rb_ref[REL_BUCKETS - 1, h]
    for d in range(bias_scr.shape[0]):
        n = row - col + d * tq
        nn = jnp.maximum(n, 0)
        nf = jnp.maximum(nn, 1).astype(F32)
        large = REL_MAX_EXACT + (jnp.log(nf / REL_MAX_EXACT) / math.log(REL_MAX_DIST / REL_MAX_EXACT)
                                 * (REL_BUCKETS - REL_MAX_EXACT)).astype(I32)
        large = jnp.minimum(large, REL_BUCKETS - 1)
        bucket = jnp.where(nn < REL_MAX_EXACT, nn, large)
        val = jnp.zeros((tq, tk), F32)
        for bb in range(REL_BUCKETS):
            val = jnp.where(bucket == bb, rb_ref[bb, h], val)
        bias_scr[d] = jnp.where(n >= 0, val - far, NEG_INF)


def _da_kernel(rb_ref, lamv_ref, q_ref, k_ref, v_ref, g_ref, o_ref,
               bias_scr, m_scr, acc_scr, *, tq, tk, lam_init):
    hp = pl.program_id(0)
    b = pl.program_id(1)
    qi = pl.program_id(2)
    ratio = tk // tq
    jd = qi // ratio
    par = qi % ratio
    hd = 2 * DA_HEAD_DIM
    heads = range(HEADS_PER_STEP)

    @pl.when((b == 0) & (qi == 0))
    def _():
        for hh in heads:
            _t5_bias_blocks(rb_ref, hp * HEADS_PER_STEP + hh, bias_scr.at[hh], tq, tk)

    scale = DA_HEAD_DIM ** -0.5
    q_all = q_ref[...]
    q2 = []
    for hh in heads:
        q = q_all[:, hh * hd:(hh + 1) * hd]
        lane = lax.broadcasted_iota(I32, q.shape, 1)
        zero = jnp.zeros_like(q)
        q2.append(jnp.concatenate([jnp.where(lane < DA_HEAD_DIM, q, zero),
                                   jnp.where(lane >= DA_HEAD_DIM, q, zero)], axis=0) * scale)
        _flash_init(m_scr.at[hh], acc_scr.at[hh])

    def block(j, bias_idx):
        off = pl.multiple_of(j * tk, tk)
        kb = k_ref[pl.ds(off, tk), :]
        vb = v_ref[pl.ds(off, tk), :]
        for hh in heads:
            s = _dot_nt(q2[hh], kb[:, hh * hd:(hh + 1) * hd])
            if bias_idx is not None:
                bias = bias_scr[hh, bias_idx]
                s = s + jnp.concatenate([bias, bias], axis=0)
            _flash_update(s, vb[:, hh * DA_V_DIM:(hh + 1) * DA_V_DIM], m_scr.at[hh], acc_scr.at[hh])

    def far_body(j, c):
        block(j, None)
        return c

    lax.fori_loop(0, jd - 1, far_body, 0)

    @pl.when((jd >= 1) & (par == 0))
    def _():
        block(jd - 1, ratio)

    @pl.when((jd >= 1) & (par != 0))
    def _():
        block(jd - 1, None)

    block(jd, par)

    lamv = lamv_ref[...]
    lam = (jnp.exp(jnp.sum(lamv[0:1] * lamv[1:2], axis=1, keepdims=True))
           - jnp.exp(jnp.sum(lamv[2:3] * lamv[3:4], axis=1, keepdims=True)) + lam_init)
    for hh in heads:
        o12 = _flash_result(acc_scr.at[hh])
        o = o12[:tq] - lam * o12[tq:]
        o = o * lax.rsqrt(jnp.mean(o * o, axis=1, keepdims=True) + NORM_EPS) * g_ref[...] * (1.0 - lam_init)
        o_ref[:, hh * DA_V_DIM:(hh + 1) * DA_V_DIM] = o.astype(o_ref.dtype)


def _da_attention(h_main, rel_bias, da_lambda, subln_g, l, lam_init, batch, seq):
    tq, tk = DA_Q_BLOCK, ATT_K_BLOCK
    ratio = tk // tq
    assert seq % tk == 0 and tk % tq == 0 and tq + 1 >= REL_MAX_DIST
    nq = seq // tq
    hw = HEADS_PER_STEP * 2 * DA_HEAD_DIM
    vw = HEADS_PER_STEP * DA_V_DIM
    kcol = DA_QK_W // hw
    vcol = 2 * DA_QK_W // vw
    return pl.pallas_call(
        functools.partial(_da_kernel, tq=tq, tk=tk, lam_init=lam_init),
        out_shape=SDS((batch * seq, DA_V_W), BF16),
        grid=(DA_HEADS // HEADS_PER_STEP, batch, nq),
        in_specs=[pl.BlockSpec(memory_space=pltpu.SMEM),
                  pl.BlockSpec((None, 4, DA_HEAD_DIM), lambda h, b, qi: (l, 0, 0)),
                  pl.BlockSpec((tq, hw), lambda h, b, qi: (b * nq + qi, h)),
                  pl.BlockSpec((seq, hw), lambda h, b, qi: (b, kcol + h)),
                  pl.BlockSpec((seq, vw), lambda h, b, qi: (b, vcol + h)),
                  pl.BlockSpec((None, 1, DA_V_DIM), lambda h, b, qi: (l, 0, 0))],
        out_specs=pl.BlockSpec((tq, vw), lambda h, b, qi: (b * nq + qi, h)),
        scratch_shapes=[pltpu.VMEM((HEADS_PER_STEP, ratio + 1, tq, tk), F32),
                        pltpu.VMEM((HEADS_PER_STEP, 2 * tq, LANES), F32),
                        pltpu.VMEM((HEADS_PER_STEP, 2 * tq, 2 * LANES), F32)],
        compiler_params=_cparams(3),
        name="da_attn",
    )(rel_bias, da_lambda, h_main, h_main, h_main, subln_g)


def _mla_kernel(q_ref, kn_ref, kr_ref, v_ref, o_ref, m_scr, acc_scr, *, blk):
    qi = pl.program_id(2)
    heads = range(HEADS_PER_STEP)
    q_all = q_ref[...]
    q = [q_all[:, hh * MLA_QCAT:(hh + 1) * MLA_QCAT] for hh in heads]
    for hh in heads:
        _flash_init(m_scr.at[hh], acc_scr.at[hh])

    def block(j, masked):
        off = pl.multiple_of(j * blk, blk)
        kn = kn_ref[pl.ds(off, blk), :]
        kr = kr_ref[pl.ds(off, blk), :]
        vb = v_ref[pl.ds(off, blk), :]
        for hh in heads:
            kcat = jnp.concatenate([kn[:, hh * MLA_NOPE_DIM:(hh + 1) * MLA_NOPE_DIM], kr], axis=1)
            s = _dot_nt(q[hh], kcat)
            if masked:
                row = lax.broadcasted_iota(I32, s.shape, 0)
                col = lax.broadcasted_iota(I32, s.shape, 1)
                s = jnp.where(col <= row, s, NEG_INF)
            _flash_update(s, vb[:, hh * MLA_V_DIM:(hh + 1) * MLA_V_DIM], m_scr.at[hh], acc_scr.at[hh])

    def far_body(j, c):
        block(j, False)
        return c

    lax.fori_loop(0, qi, far_body, 0)
    block(qi, True)
    for hh in heads:
        o_ref[:, hh * MLA_V_DIM:(hh + 1) * MLA_V_DIM] = _flash_result(acc_scr.at[hh]).astype(o_ref.dtype)


def _mla_attention(q_cat, kv, kr2, batch, seq):
    blk = min(ATT_K_BLOCK, seq)
    nq = seq // blk
    hps = HEADS_PER_STEP
    vcol = MLA_HEADS // hps
    return pl.pallas_call(
        functools.partial(_mla_kernel, blk=blk),
        out_shape=SDS((batch * seq, MLA_O_W), BF16),
        grid=(batch, MLA_HEADS // hps, nq),
        in_specs=[pl.BlockSpec((blk, hps * MLA_QCAT), lambda b, h, qi: (b * nq + qi, h)),
                  pl.BlockSpec((seq, hps * MLA_NOPE_DIM), lambda b, h, qi: (b, h)),
                  pl.BlockSpec((seq, LANES), lambda b, h, qi: (b, 0)),
                  pl.BlockSpec((seq, hps * MLA_V_DIM), lambda b, h, qi: (b, vcol + h))],
        out_specs=pl.BlockSpec((blk, hps * MLA_V_DIM), lambda b, h, qi: (b * nq + qi, h)),
        scratch_shapes=[pltpu.VMEM((hps, blk, LANES), F32),
                        pltpu.VMEM((hps, blk, 2 * LANES), F32)],
        compiler_params=_cparams(3),
        name="mla_attn",
    )(q_cat, kv, kr2, kv)


def _sigmoid(x):
    return 1.0 / (1.0 + jnp.exp(-x))


def _gated_kernel(oa_ref, ob_ref, wa_ref, wb_ref, ga_ref, gb_ref, o_ref):
    ya = jnp.dot(oa_ref[...], wa_ref[...].astype(BF16), preferred_element_type=F32)
    yb = jnp.dot(ob_ref[...], wb_ref[...].astype(BF16), preferred_element_type=F32)
    y = _sigmoid(ga_ref[...].astype(F32)) * ya + _sigmoid(gb_ref[...].astype(F32)) * yb
    o_ref[...] = y.astype(o_ref.dtype)


def _gated(o_a, o_b, w_a, w_b, h_g, l, tm, tn):
    m = o_a.shape[0]
    d = w_a.shape[2]
    tm = min(tm, m)
    ngb = d // tn
    return pl.pallas_call(
        _gated_kernel,
        out_shape=SDS((m, d), BF16),
        grid=(m // tm, ngb),
        in_specs=[pl.BlockSpec((tm, DA_V_W), lambda i, j: (i, 0)),
                  pl.BlockSpec((tm, MLA_O_W), lambda i, j: (i, 0)),
                  pl.BlockSpec((None, DA_V_W, tn), lambda i, j: (l, 0, j)),
                  pl.BlockSpec((None, MLA_O_W, tn), lambda i, j: (l, 0, j)),
                  pl.BlockSpec((tm, tn), lambda i, j: (i, j)),
                  pl.BlockSpec((tm, tn), lambda i, j: (i, ngb + j))],
        out_specs=pl.BlockSpec((tm, tn), lambda i, j: (i, j)),
        compiler_params=_cparams(2),
        name="gated",
    )(o_a, o_b, w_a, w_b, h_g, h_g)


def _layer_norm(z, g, b):
    mu = jnp.mean(z, axis=-1, keepdims=True)
    zc = z - mu
    var = jnp.mean(zc * zc, axis=-1, keepdims=True)
    return zc * lax.rsqrt(var + NORM_EPS) * g + b


def _split_bf16(x):
    hi = x.astype(BF16)
    lo = (x - hi.astype(F32)).astype(BF16)
    return hi, lo


def _route_record(lg):
    lane = lax.broadcasted_iota(I32, lg.shape, 1)
    lane_f = lane.astype(F32)
    big = float(ROUTE_W)
    gmask = lane < N_GROUPS
    g_max = jnp.max(jnp.where(gmask, lg, NEG_INF), axis=1, keepdims=True)
    g_idx = jnp.min(jnp.where(gmask & (lg == g_max), lane_f, big), axis=1, keepdims=True)
    g_w = 1.0 / jnp.sum(jnp.where(gmask, jnp.exp(lg - g_max), 0.0), axis=1, keepdims=True)

    lo = N_GROUPS + EXPERTS_PER_GROUP * g_idx
    sel = (lane_f >= lo) & (lane_f < lo + EXPERTS_PER_GROUP)
    e_max = jnp.max(jnp.where(sel, lg, NEG_INF), axis=1, keepdims=True)
    pe = jnp.where(sel, jnp.exp(lg - e_max), 0.0)
    prob = pe / jnp.sum(pe, axis=1, keepdims=True)
    p1 = jnp.max(jnp.where(sel, prob, -1.0), axis=1, keepdims=True)
    i1 = jnp.min(jnp.where(sel & (prob == p1), lane_f, big), axis=1, keepdims=True)
    sel2 = sel & (lane_f != i1)
    p2 = jnp.max(jnp.where(sel2, prob, -1.0), axis=1, keepdims=True)
    i2 = jnp.min(jnp.where(sel2 & (prob == p2), lane_f, big), axis=1, keepdims=True)
    tot = p1 + p2
    w1 = g_w * (p1 / tot)
    w2 = g_w * (p2 / tot)
    rec = jnp.where(lane == 0, i1 - N_GROUPS,
          jnp.where(lane == 1, i2 - N_GROUPS,
          jnp.where(lane == 2, w1, jnp.where(lane == 3, w2, 0.0))))
    return rec


def _outproj_kernel(y_ref, w_ref, x_ref, g_ref, b_ref, wr_ref, br_ref,
                    xo_ref, xt_ref, route_ref, *, alpha):
    mix = jnp.dot(y_ref[...], w_ref[...].astype(BF16), preferred_element_type=F32)
    xn = _layer_norm(alpha * x_ref[...] + mix, g_ref[...], b_ref[...])
    xo_ref[...] = xn
    _wide_to_tiled(xn, xt_ref)
    xh, xl = _split_bf16(xn)
    wh, wl = _split_bf16(wr_ref[...])
    lg = (jnp.dot(xh, wh, preferred_element_type=F32) + jnp.dot(xl, wh, preferred_element_type=F32)
          + jnp.dot(xh, wl, preferred_element_type=F32) + br_ref[...])
    route_ref[...] = _route_record(lg)


def _outproj_ln_route(y, w_out, x, ln_g, ln_b, w_r, b_r, l, alpha, tm):
    m, d = x.shape
    tm = min(tm, m)
    const = dict(pipeline_mode=pl.Buffered(1))
    return pl.pallas_call(
        functools.partial(_outproj_kernel, alpha=alpha),
        out_shape=(SDS((m, d), F32), SDS((m * ROW_TILES, LANES), F32), SDS((m, ROUTE_W), F32)),
        grid=(m // tm,),
        in_specs=[pl.BlockSpec((tm, d), lambda i: (i, 0)),
                  pl.BlockSpec((None, d, d), lambda i: (l, 0, 0), **const),
                  pl.BlockSpec((tm, d), lambda i: (i, 0)),
                  pl.BlockSpec((None, 1, d), lambda i: (l, 0, 0)),
                  pl.BlockSpec((None, 1, d), lambda i: (l, 0, 0)),
                  pl.BlockSpec((None, d, ROUTE_W), lambda i: (0, 0, 0)),
                  pl.BlockSpec((None, 1, ROUTE_W), lambda i: (0, 0, 0))],
        out_specs=(pl.BlockSpec((tm, d), lambda i: (i, 0)),
                   pl.BlockSpec((tm * ROW_TILES, LANES), lambda i: (i, 0)),
                   pl.BlockSpec((tm, ROUTE_W), lambda i: (i, 0))),
        compiler_params=_cparams(1),
        name="outproj_ln_route",
    )(y, w_out, x, ln_g, ln_b, w_r, b_r)


def _row_copy(src_hbm, row, dst, r, sem):
    src = src_hbm.at[pl.ds(pl.multiple_of(row * ROW_TILES, ROW_TILES), ROW_TILES), :]
    return pltpu.make_async_copy(src, dst.at[pl.ds(pl.multiple_of(r * ROW_TILES, ROW_TILES), ROW_TILES), :], sem)


def _tiled_to_wide(ref, n):
    return jnp.concatenate([ref[pl.ds(s, n, stride=ROW_TILES), :] for s in range(ROW_TILES)], axis=1)


def _wide_to_tiled(val, ref):
    n = val.shape[0]
    for s in range(ROW_TILES):
        ref[pl.ds(s, n, stride=ROW_TILES), :] = val[:, s * LANES:(s + 1) * LANES]


def _gather_rows(src_hbm, idx_ref, base, dst, sem, n):
    assert n % GATHER_UNROLL == 0

    def issue(i, c):
        for u in range(GATHER_UNROLL):
            r = i * GATHER_UNROLL + u
            _row_copy(src_hbm, idx_ref[base + r], dst, r, sem).start(priority=u % DMA_QUEUES)
        return c

    lax.fori_loop(0, n // GATHER_UNROLL, issue, 0)


def _gather_wait(src_hbm, dst, sem, n):
    def wait(r, c):
        _row_copy(src_hbm, 0, dst, r, sem).wait()
        return c

    lax.fori_loop(0, n, wait, 0, unroll=GATHER_UNROLL)


def _moe_kernel(bs_ref, nb_ref, nu_ref, tok_ref, x_hbm, wg_ref, wu_ref, wd_ref, y_hbm,
                xbuf, obuf, gsem, osem, *, bm, n_blocks):
    e = pl.program_id(0)
    n_used = nu_ref[0]

    def out_copy(g, slot):
        rows = pl.ds(pl.multiple_of(g * (bm * ROW_TILES), bm * ROW_TILES), bm * ROW_TILES)
        return pltpu.make_async_copy(obuf.at[slot], y_hbm.at[rows, :], osem.at[slot])

    @pl.when(e == 0)
    def _():
        _gather_rows(x_hbm, tok_ref, 0, xbuf.at[0], gsem.at[0], bm)

    def body(j, c):
        g = bs_ref[e] + j
        slot = lax.rem(g, 2)

        _gather_wait(x_hbm, xbuf.at[slot], gsem.at[slot], bm)

        @pl.when(g >= 2)
        def _():
            out_copy(g - 2, slot).wait()

        g_next = jnp.minimum(g + 1, n_blocks - 1)
        for r in range(bm):
            _row_copy(x_hbm, tok_ref[g_next * bm + r], xbuf.at[1 - slot], r,
                      gsem.at[1 - slot]).start(priority=r % DMA_QUEUES)

        xb = _tiled_to_wide(xbuf.at[slot], bm).astype(BF16)
        gt = _dot_nt(xb, wg_ref[...].astype(BF16))
        up = _dot_nt(xb, wu_ref[...].astype(BF16))
        hid = ((gt * _sigmoid(gt)) * up).astype(BF16)
        _wide_to_tiled(jnp.dot(hid, wd_ref[...].astype(BF16), preferred_element_type=F32),
                       obuf.at[slot])
        out_copy(g, slot).start()
        return c

    lax.fori_loop(0, nb_ref[e], body, 0)

    @pl.when(e == pl.num_programs(0) - 1)
    def _():
        @pl.when(n_used >= 2)
        def _():
            out_copy(n_used - 2, lax.rem(n_used - 2, 2)).wait()

        out_copy(n_used - 1, lax.rem(n_used - 1, 2)).wait()
        _gather_wait(x_hbm, xbuf.at[lax.rem(n_used, 2)], gsem.at[lax.rem(n_used, 2)], bm)

        obuf[0] = jnp.zeros(obuf.shape[1:], obuf.dtype)

        def fill(g, c):
            cp = out_copy(g, 0)
            cp.start()
            cp.wait()
            return c

        lax.fori_loop(n_used, n_blocks, fill, 0)


def _moe_ffn(x_tiled, bstart, nblk, n_used, row_tok, wgt, wut, wd, l, bm):
    f, d = wd.shape[2], wd.shape[3]
    assert d == ROW_TILES * LANES
    n_blocks = row_tok.shape[0] // bm
    w_spec = pl.BlockSpec((None, None, f, d), lambda e, bs, nb, nu, tok: (l, e, 0, 0))
    grid_spec = pltpu.PrefetchScalarGridSpec(
        num_scalar_prefetch=4,
        grid=(N_EXPERTS,),
        in_specs=[pl.BlockSpec(memory_space=pl.ANY), w_spec, w_spec, w_spec],
        out_specs=pl.BlockSpec(memory_space=pl.ANY),
        scratch_shapes=[pltpu.VMEM((2, bm * ROW_TILES, LANES), F32),
                        pltpu.VMEM((2, bm * ROW_TILES, LANES), F32),
                        pltpu.SemaphoreType.DMA((2,)), pltpu.SemaphoreType.DMA((2,))],
    )
    return pl.pallas_call(
        functools.partial(_moe_kernel, bm=bm, n_blocks=n_blocks),
        out_shape=SDS((n_blocks * bm * ROW_TILES, LANES), F32),
        grid_spec=grid_spec,
        compiler_params=_cparams(1),
        name="moe_ffn",
    )(bstart, nblk, n_used, row_tok, x_tiled, wgt, wut, wd)


def _combine_kernel(pos_ref, yb_hbm, x_ref, route_ref, g_ref, b_ref, xo_ref, xbo_ref,
                    buf, sems, *, tm, t_total, alpha):
    i = pl.program_id(0)
    slot = lax.rem(i, 2)

    def gather(step, s):
        for k in range(TOP_K):
            _gather_rows(yb_hbm, pos_ref, k * t_total + step * tm, buf.at[s, k],
                         sems.at[s * TOP_K + k], tm)

    @pl.when(i == 0)
    def _():
        gather(0, 0)

    @pl.when(i + 1 < pl.num_programs(0))
    def _():
        gather(i + 1, 1 - slot)

    for k in range(TOP_K):
        _gather_wait(yb_hbm, buf.at[slot, k], sems.at[slot * TOP_K + k], tm)

    rec = route_ref[...]
    z = (alpha * x_ref[...] + rec[:, 2:3] * _tiled_to_wide(buf.at[slot, 0], tm)
         + rec[:, 3:4] * _tiled_to_wide(buf.at[slot, 1], tm))
    xn = _layer_norm(z, g_ref[...], b_ref[...])
    xo_ref[...] = xn
    xbo_ref[...] = xn.astype(BF16)


def _combine_ln(pos_k, yb, x, route, ln_g, ln_b, l, alpha, tm):
    t, d = x.shape
    tm = min(tm, t)
    grid_spec = pltpu.PrefetchScalarGridSpec(
        num_scalar_prefetch=1,
        grid=(t // tm,),
        in_specs=[pl.BlockSpec(memory_space=pl.ANY),
                  pl.BlockSpec((tm, d), lambda i, pos: (i, 0)),
                  pl.BlockSpec((tm, ROUTE_W), lambda i, pos: (i, 0)),
                  pl.BlockSpec((None, 1, d), lambda i, pos: (l, 0, 0)),
                  pl.BlockSpec((None, 1, d), lambda i, pos: (l, 0, 0))],
        out_specs=(pl.BlockSpec((tm, d), lambda i, pos: (i, 0)),
                   pl.BlockSpec((tm, d), lambda i, pos: (i, 0))),
        scratch_shapes=[pltpu.VMEM((2, TOP_K, tm * ROW_TILES, LANES), F32),
                        pltpu.SemaphoreType.DMA((2 * TOP_K,))],
    )
    return pl.pallas_call(
        functools.partial(_combine_kernel, tm=tm, t_total=t, alpha=alpha),
        out_shape=(SDS((t, d), F32), SDS((t, d), BF16)),
        grid_spec=grid_spec,
        compiler_params=_cparams(1),
        name="combine_ln",
    )(pos_k, yb, x, route, ln_g, ln_b)


def _route_meta(e_ids, bm):
    t = e_ids.shape[0]
    a = t * TOP_K
    flat_e = e_ids.reshape(a)
    onehot = (flat_e[:, None] == jnp.arange(N_EXPERTS, dtype=I32)[None, :]).astype(I32)
    csum = jnp.cumsum(onehot, axis=0)
    rank = jnp.take_along_axis(csum, flat_e[:, None], axis=1)[:, 0] - 1
    counts = csum[-1]
    nblk = (counts + bm - 1) // bm
    bend = jnp.cumsum(nblk)
    bstart = bend - nblk
    dest = (bstart[flat_e] * bm + rank).astype(I32)
    n_blocks = a // bm + N_EXPERTS
    row_tok = jnp.zeros((n_blocks * bm,), I32).at[dest].set(jnp.arange(a, dtype=I32) // TOP_K)
    pos_k = dest.reshape(t, TOP_K).T.reshape(a)
    return pos_k, row_tok, bstart.astype(I32), nblk.astype(I32), bend[-1:].astype(I32)


def _swap_halves(w, axis):
    half = w.shape[axis] // 2
    lo = lax.slice_in_dim(w, 0, half, axis=axis)
    hi = lax.slice_in_dim(w, half, 2 * half, axis=axis)
    return jnp.concatenate([-hi, lo], axis=axis)


def _prep_layer_weights(wk_t, w_uq_l, w_ukv_l, w_rg_l, b_rg_l, w_re_l, b_re_l):
    d = wk_t.shape[1]
    w_krt = jnp.concatenate([wk_t, _swap_halves(wk_t, 0)], axis=0)
    wq = w_uq_l.reshape(MLA_Q_RANK, MLA_HEADS, MLA_NOPE_DIM + MLA_ROPE_DIM)
    wq_rope = wq[..., MLA_NOPE_DIM:]
    w_uq2 = jnp.concatenate([wq[..., :MLA_NOPE_DIM], wq_rope, _swap_halves(wq_rope, 2)],
                            axis=-1).reshape(MLA_Q_RANK, MLA_HEADS * MLA_QCAT)
    wkv = w_ukv_l.reshape(MLA_KV_RANK, MLA_HEADS, MLA_NOPE_DIM + MLA_V_DIM)
    w_ukv2 = jnp.concatenate([wkv[..., :MLA_NOPE_DIM].reshape(MLA_KV_RANK, -1),
                              wkv[..., MLA_NOPE_DIM:].reshape(MLA_KV_RANK, -1)], axis=1)
    pad = ROUTE_W - N_GROUPS - N_EXPERTS
    w_r = jnp.concatenate([w_rg_l, w_re_l, jnp.zeros((d, pad), F32)], axis=1)[None]
    b_r = jnp.concatenate([b_rg_l, b_re_l, jnp.zeros((pad,), F32)])[None, None]
    return w_krt, w_uq2, w_ukv2, w_r, b_r


def _rope_tables(seq):
    inv = ROPE_THETA ** (-jnp.arange(0, MLA_ROPE_DIM, 2, dtype=F32) / MLA_ROPE_DIM)
    ang = jnp.arange(seq, dtype=F32)[:, None] * inv[None, :]
    return jnp.cos(ang), jnp.sin(ang)


def kernel(x, w_in, da_lambda, da_subln_g, mla_q_norm_g, mla_w_uq, mla_kv_norm_g, mla_w_ukv, w_branch_a, w_branch_b, w_out, rel_bias, ln1_g, ln1_b, router_w_group, router_b_group, router_w_expert, router_b_expert, expert_w_gate, expert_w_up, expert_w_down, ln2_g, ln2_b):
    batch, seq, d = x.shape
    depth = w_in.shape[0]
    t = batch * seq
    alpha = (2 * depth) ** 0.25
    main_w = DA_QK_W * 2 + DA_V_W + MLA_Q_RANK + MLA_KV_RANK
    gate0 = main_w + MLA_ROPE_DIM

    cos, sin = _rope_tables(seq)
    tabk = jnp.concatenate([cos, cos, sin, sin], axis=1)
    q_scale = (MLA_NOPE_DIM + MLA_ROPE_DIM) ** -0.5
    tabq = q_scale * jnp.concatenate([jnp.ones((seq, MLA_NOPE_DIM), F32), tabk], axis=1)

    subln_g = da_subln_g[:, None, :]
    ln1_g3, ln1_b3 = ln1_g[:, None, :], ln1_b[:, None, :]
    ln2_g3, ln2_b3 = ln2_g[:, None, :], ln2_b[:, None, :]

    wt_in = jnp.swapaxes(w_in, 1, 2)
    wgt = jnp.swapaxes(expert_w_gate, 2, 3)
    wut = jnp.swapaxes(expert_w_up, 2, 3)

    xf = x.reshape(t, d)
    xb = xf.astype(BF16)
    for l in range(depth):
        lam_init = 0.8 - 0.6 * math.exp(-0.3 * l)
        w_krt, w_uq2, w_ukv2, w_r, b_r = _prep_layer_weights(
            wt_in[l, main_w:gate0, :], mla_w_uq[l], mla_w_ukv[l], router_w_group[l],
            router_b_group[l], router_w_expert[l], router_b_expert[l])

        h_main = _matmul_nt(xb, wt_in, l, 0, main_w, 4096, 256, BF16, "mm_main")
        h_g = _matmul_nt(xb, wt_in, l, gate0, 2 * d, 2048, 512, BF16, "mm_gates")
        kr2 = _krope(xb, w_krt, tabk, seq, 512)
        o_a = _da_attention(h_main, rel_bias, da_lambda, subln_g, l, lam_init, batch, seq)
        q_cat = _uq(h_main, mla_q_norm_g[l][None], w_uq2, tabq, seq, 512)
        kv = _ukv(h_main, mla_kv_norm_g[l][None], w_ukv2, 512)
        o_b = _mla_attention(q_cat, kv, kr2, batch, seq)
        y = _gated(o_a, o_b, w_branch_a, w_branch_b, h_g, l, 1024, 512)
        x1, x1_tiled, route = _outproj_ln_route(y, w_out, xf, ln1_g3, ln1_b3, w_r, b_r, l, alpha, 256)

        e_ids = route[:, :TOP_K].astype(I32)
        pos_k, row_tok, bstart, nblk, n_used = _route_meta(e_ids, MOE_BLOCK)
        yb = _moe_ffn(x1_tiled, bstart, nblk, n_used, row_tok, wgt, wut, expert_w_down, l, MOE_BLOCK)
        xf, xb = _combine_ln(pos_k, yb, x1, route, ln2_g3, ln2_b3, l, alpha, 256)
    return xf.reshape(batch, seq, d)
```

```python
import functools
import math

import jax
import jax.numpy as jnp
from jax import lax
from jax.experimental import pallas as pl
from jax.experimental.pallas import tpu as pltpu

F32 = jnp.float32
BF16 = jnp.bfloat16
I32 = jnp.int32
U32 = jnp.uint32
SDS = jax.ShapeDtypeStruct

DA_HEADS = 8
DA_HEAD_DIM = 64
DA_V_DIM = 2 * DA_HEAD_DIM
MLA_HEADS = 8
MLA_Q_RANK = 768
MLA_KV_RANK = 512
MLA_NOPE_DIM = 128
MLA_ROPE_DIM = 64
MLA_V_DIM = 128
ROPE_THETA = 10000.0
REL_BUCKETS = 32
REL_MAX_DIST = 128
REL_MAX_EXACT = REL_BUCKETS // 2
N_GROUPS = 4
EXPERTS_PER_GROUP = 8
N_EXPERTS = N_GROUPS * EXPERTS_PER_GROUP
TOP_K = 2
NORM_EPS = 1e-5
NEG_INF = -1e30

DA_QK_W = DA_HEADS * 2 * DA_HEAD_DIM
DA_V_W = DA_HEADS * DA_V_DIM
MLA_O_W = MLA_HEADS * MLA_V_DIM
MLA_QCAT = 2 * MLA_NOPE_DIM

LANES = 128
SUBLANES = 8
VMEM_LIMIT = 56 * 1024 * 1024

ATT_K_BLOCK = 512
DA_Q_BLOCK = 256
MOE_BLOCK = 256
ROUTE_W = LANES
GATHER_UNROLL = 8
ROW_TILES = 8
HEADS_PER_STEP = 4
DMA_QUEUES = 2


def _cparams(n_axes):
    return pltpu.CompilerParams(dimension_semantics=("arbitrary",) * n_axes,
                                vmem_limit_bytes=VMEM_LIMIT)


def _dot_nt(a, b):
    return lax.dot_general(a, b, (((1,), (1,)), ((), ())), preferred_element_type=F32)


def _mm_nt_kernel(a_ref, w_ref, o_ref):
    o_ref[...] = _dot_nt(a_ref[...], w_ref[0].astype(BF16)).astype(o_ref.dtype)


def _matmul_nt(a, wt3, l, row0, n_rows, tm, tn, out_dtype, name):
    m, k = a.shape
    tm = min(tm, m)
    assert m % tm == 0 and n_rows % tn == 0
    if row0 % tn == 0:
        r0 = row0 // tn
        w_spec = pl.BlockSpec((1, tn, k), lambda i, j: (l, r0 + j, 0))
    else:
        w_spec = pl.BlockSpec((pl.Element(1), pl.Element(tn), pl.Element(k)),
                              lambda i, j: (l, pl.multiple_of(row0 + j * tn, SUBLANES), 0))
    return pl.pallas_call(
        _mm_nt_kernel,
        out_shape=SDS((m, n_rows), out_dtype),
        grid=(m // tm, n_rows // tn),
        in_specs=[pl.BlockSpec((tm, k), lambda i, j: (i, 0)), w_spec],
        out_specs=pl.BlockSpec((tm, tn), lambda i, j: (i, j)),
        compiler_params=_cparams(2),
        name=name,
    )(a, wt3)


def _krope_kernel(a_ref, w_ref, tab_ref, o_ref):
    t = _dot_nt(a_ref[...], w_ref[...].astype(BF16))
    t = t * tab_ref[...]
    o_ref[...] = (t + pltpu.roll(t, MLA_ROPE_DIM, axis=1)).astype(o_ref.dtype)


def _krope(xb, w_kr, tabk, seq, tm):
    m, k = xb.shape
    tm = min(tm, seq)
    nsb = seq // tm
    return pl.pallas_call(
        _krope_kernel,
        out_shape=SDS((m, LANES), BF16),
        grid=(m // tm,),
        in_specs=[pl.BlockSpec((tm, k), lambda i: (i, 0)),
                  pl.BlockSpec((LANES, k), lambda i: (0, 0)),
                  pl.BlockSpec((tm, LANES), lambda i: (i % nsb, 0))],
        out_specs=pl.BlockSpec((tm, LANES), lambda i: (i, 0)),
        compiler_params=_cparams(1),
        name="krope",
    )(xb, w_kr, tabk)


def _rms(c, g):
    return c * lax.rsqrt(jnp.mean(c * c, axis=-1, keepdims=True) + NORM_EPS) * g


def _uq_kernel(c_ref, g_ref, w_ref, tab_ref, o_ref):
    n = _rms(c_ref[...].astype(F32), g_ref[...])
    acc = jnp.dot(n.astype(BF16), w_ref[...].astype(BF16), preferred_element_type=F32)
    tab = tab_ref[...]
    for h in range(MLA_HEADS):
        sl = slice(h * MLA_QCAT, (h + 1) * MLA_QCAT)
        o_ref[:, sl] = (acc[:, sl] * tab).astype(o_ref.dtype)


def _uq(h_main, g, w_uq2, tabq, seq, tm):
    m = h_main.shape[0]
    tm = min(tm, seq)
    nsb = seq // tm
    cq_blk = (DA_QK_W * 2 + DA_V_W) // MLA_Q_RANK
    n_out = MLA_HEADS * MLA_QCAT
    return pl.pallas_call(
        _uq_kernel,
        out_shape=SDS((m, n_out), BF16),
        grid=(m // tm,),
        in_specs=[pl.BlockSpec((tm, MLA_Q_RANK), lambda i: (i, cq_blk)),
                  pl.BlockSpec((1, MLA_Q_RANK), lambda i: (0, 0)),
                  pl.BlockSpec((MLA_Q_RANK, n_out), lambda i: (0, 0)),
                  pl.BlockSpec((tm, MLA_QCAT), lambda i: (i % nsb, 0))],
        out_specs=pl.BlockSpec((tm, n_out), lambda i: (i, 0)),
        compiler_params=_cparams(1),
        name="mla_uq",
    )(h_main, g, w_uq2, tabq)


def _ukv_kernel(c0_ref, c1_ref, g_ref, w_ref, o_ref):
    c = jnp.concatenate([c0_ref[...], c1_ref[...]], axis=1).astype(F32)
    n = _rms(c, g_ref[...])
    o_ref[...] = jnp.dot(n.astype(BF16), w_ref[...].astype(BF16),
                         preferred_element_type=F32).astype(o_ref.dtype)


def _ukv(h_main, g, w_ukv2, tm):
    m = h_main.shape[0]
    tm = min(tm, m)
    half = MLA_KV_RANK // 2
    b0 = (DA_QK_W * 2 + DA_V_W + MLA_Q_RANK) // half
    n_out = w_ukv2.shape[1]
    return pl.pallas_call(
        _ukv_kernel,
        out_shape=SDS((m, n_out), BF16),
        grid=(m // tm,),
        in_specs=[pl.BlockSpec((tm, half), lambda i: (i, b0)),
                  pl.BlockSpec((tm, half), lambda i: (i, b0 + 1)),
                  pl.BlockSpec((1, MLA_KV_RANK), lambda i: (0, 0)),
                  pl.BlockSpec((MLA_KV_RANK, n_out), lambda i: (0, 0))],
        out_specs=pl.BlockSpec((tm, n_out), lambda i: (i, 0)),
        compiler_params=_cparams(1),
        name="mla_ukv",
    )(h_main, h_main, g, w_ukv2)


def _flash_init(m_scr, acc_scr):
    m_scr[...] = jnp.full(m_scr.shape, NEG_INF, F32)
    acc_scr[...] = jnp.zeros(acc_scr.shape, F32)


def _flash_update(s, v, m_scr, acc_scr):
    v1 = jnp.concatenate([v, jnp.ones(v.shape, v.dtype)], axis=1)
    m_prev = m_scr[...]
    m_new = jnp.maximum(m_prev, jnp.max(s, axis=1, keepdims=True))
    p = jnp.exp(s - jnp.concatenate([m_new] * (s.shape[1] // LANES), axis=1))
    alpha = jnp.exp(m_prev - m_new)
    acc_scr[...] = (jnp.concatenate([alpha, alpha], axis=1) * acc_scr[...]
                    + jnp.dot(p.astype(BF16), v1, preferred_element_type=F32))
    m_scr[...] = m_new


def _flash_result(acc_scr):
    acc = acc_scr[...]
    return acc[:, :LANES] / acc[:, LANES:]


def _t5_bias_blocks(rb_ref, h, bias_scr, tq, tk):
    row = lax.broadcasted_iota(I32, (tq, tk), 0)
    col = lax.broadcasted_iota(I32, (tq, tk), 1)
    far = rb_ref[REL_BUCKETS - 1, h]
    for d in range(bias_scr.shape[0]):
        n = row - col + d * tq
        nn = jnp.maximum(n, 0)
        nf = jnp.maximum(nn, 1).astype(F32)
        large = REL_MAX_EXACT + (jnp.log(nf / REL_MAX_EXACT) / math.log(REL_MAX_DIST / REL_MAX_EXACT)
                                 * (REL_BUCKETS - REL_MAX_EXACT)).astype(I32)
        large = jnp.minimum(large, REL_BUCKETS - 1)
        bucket = jnp.where(nn < REL_MAX_EXACT, nn, large)
        val = jnp.zeros((tq, tk), F32)
        for bb in range(REL_BUCKETS):
            val = jnp.where(bucket == bb, rb_ref[bb, h], val)
        bias_scr[d] = jnp.where(n >= 0, val - far, NEG_INF)


def _da_kernel(rb_ref, lamv_ref, q_ref, k_ref, v_ref, g_ref, o_ref,
               bias_scr, m_scr, acc_scr, *, tq, tk, lam_init):
    hp = pl.program_id(0)
    b = pl.program_id(1)
    qi = pl.program_id(2)
    ratio = tk // tq
    jd = qi // ratio
    par = qi % ratio
    hd = 2 * DA_HEAD_DIM
    heads = range(HEADS_PER_STEP)

    @pl.when((b == 0) & (qi == 0))
    def _():
        for hh in heads:
            _t5_bias_blocks(rb_ref, hp * HEADS_PER_STEP + hh, bias_scr.at[hh], tq, tk)

    scale = DA_HEAD_DIM ** -0.5
    q_all = q_ref[...]
    q2 = []
    for hh in heads:
        q = q_all[:, hh * hd:(hh + 1) * hd]
        lane = lax.broadcasted_iota(I32, q.shape, 1)
        zero = jnp.zeros_like(q)
        q2.append(jnp.concatenate([jnp.where(lane < DA_HEAD_DIM, q, zero),
                                   jnp.where(lane >= DA_HEAD_DIM, q, zero)], axis=0) * scale)
        _flash_init(m_scr.at[hh], acc_scr.at[hh])

    def block(j, bias_idx):
        off = pl.multiple_of(j * tk, tk)
        kb = k_ref[pl.ds(off, tk), :]
        vb = v_ref[pl.ds(off, tk), :]
        for hh in heads:
            s = _dot_nt(q2[hh], kb[:, hh * hd:(hh + 1) * hd])
            if bias_idx is not None:
                bias = bias_scr[hh, bias_idx]
                s = s + jnp.concatenate([bias, bias], axis=0)
            _flash_update(s, vb[:, hh * DA_V_DIM:(hh + 1) * DA_V_DIM], m_scr.at[hh], acc_scr.at[hh])

    def far_body(j, c):
        block(j, None)
        return c

    lax.fori_loop(0, jd - 1, far_body, 0)

    @pl.when((jd >= 1) & (par == 0))
    def _():
        block(jd - 1, ratio)

    @pl.when((jd >= 1) & (par != 0))
    def _():
        block(jd - 1, None)

    block(jd, par)

    lamv = lamv_ref[...]
    lam = (jnp.exp(jnp.sum(lamv[0:1] * lamv[1:2], axis=1, keepdims=True))
           - jnp.exp(jnp.sum(lamv[2:3] * lamv[3:4], axis=1, keepdims=True)) + lam_init)
    for hh in heads:
        o12 = _flash_result(acc_scr.at[hh])
        o = o12[:tq] - lam * o12[tq:]
        o = o * lax.rsqrt(jnp.mean(o * o, axis=1, keepdims=True) + NORM_EPS) * g_ref[...] * (1.0 - lam_init)
        o_ref[:, hh * DA_V_DIM:(hh + 1) * DA_V_DIM] = o.astype(o_ref.dtype)


def _da_attention(h_main, rel_bias, da_lambda, subln_g, l, lam_init, batch, seq):
    tq, tk = DA_Q_BLOCK, ATT_K_BLOCK
    ratio = tk // tq
    assert seq % tk == 0 and tk % tq == 0 and tq + 1 >= REL_MAX_DIST
    nq = seq // tq
    hw = HEADS_PER_STEP * 2 * DA_HEAD_DIM
    vw = HEADS_PER_STEP * DA_V_DIM
    kcol = DA_QK_W // hw
    vcol = 2 * DA_QK_W // vw
    return pl.pallas_call(
        functools.partial(_da_kernel, tq=tq, tk=tk, lam_init=lam_init),
        out_shape=SDS((batch * seq, DA_V_W), BF16),
        grid=(DA_HEADS // HEADS_PER_STEP, batch, nq),
        in_specs=[pl.BlockSpec(memory_space=pltpu.SMEM),
                  pl.BlockSpec((None, 4, DA_HEAD_DIM), lambda h, b, qi: (l, 0, 0)),
                  pl.BlockSpec((tq, hw), lambda h, b, qi: (b * nq + qi, h)),
                  pl.BlockSpec((seq, hw), lambda h, b, qi: (b, kcol + h)),
                  pl.BlockSpec((seq, vw), lambda h, b, qi: (b, vcol + h)),
                  pl.BlockSpec((None, 1, DA_V_DIM), lambda h, b, qi: (l, 0, 0))],
        out_specs=pl.BlockSpec((tq, vw), lambda h, b, qi: (b * nq + qi, h)),
        scratch_shapes=[pltpu.VMEM((HEADS_PER_STEP, ratio + 1, tq, tk), F32),
                        pltpu.VMEM((HEADS_PER_STEP, 2 * tq, LANES), F32),
                        pltpu.VMEM((HEADS_PER_STEP, 2 * tq, 2 * LANES), F32)],
        compiler_params=_cparams(3),
        name="da_attn",
    )(rel_bias, da_lambda, h_main, h_main, h_main, subln_g)


def _mla_kernel(q_ref, kn_ref, kr_ref, v_ref, o_ref, m_scr, acc_scr, *, blk):
    qi = pl.program_id(2)
    heads = range(HEADS_PER_STEP)
    q_all = q_ref[...]
    q = [q_all[:, hh * MLA_QCAT:(hh + 1) * MLA_QCAT] for hh in heads]
    for hh in heads:
        _flash_init(m_scr.at[hh], acc_scr.at[hh])

    def block(j, masked):
        off = pl.multiple_of(j * blk, blk)
        kn = kn_ref[pl.ds(off, blk), :]
        kr = kr_ref[pl.ds(off, blk), :]
        vb = v_ref[pl.ds(off, blk), :]
        for hh in heads:
            kcat = jnp.concatenate([kn[:, hh * MLA_NOPE_DIM:(hh + 1) * MLA_NOPE_DIM], kr], axis=1)
            s = _dot_nt(q[hh], kcat)
            if masked:
                row = lax.broadcasted_iota(I32, s.shape, 0)
                col = lax.broadcasted_iota(I32, s.shape, 1)
                s = jnp.where(col <= row, s, NEG_INF)
            _flash_update(s, vb[:, hh * MLA_V_DIM:(hh + 1) * MLA_V_DIM], m_scr.at[hh], acc_scr.at[hh])

    def far_body(j, c):
        block(j, False)
        return c

    lax.fori_loop(0, qi, far_body, 0)
    block(qi, True)
    for hh in heads:
        o_ref[:, hh * MLA_V_DIM:(hh + 1) * MLA_V_DIM] = _flash_result(acc_scr.at[hh]).astype(o_ref.dtype)


def _mla_attention(q_cat, kv, kr2, batch, seq):
    blk = min(ATT_K_BLOCK, seq)
    nq = seq // blk
    hps = HEADS_PER_STEP
    vcol = MLA_HEADS // hps
    return pl.pallas_call(
        functools.partial(_mla_kernel, blk=blk),
        out_shape=SDS((batch * seq, MLA_O_W), BF16),
        grid=(batch, MLA_HEADS // hps, nq),
        in_specs=[pl.BlockSpec((blk, hps * MLA_QCAT), lambda b, h, qi: (b * nq + qi, h)),
                  pl.BlockSpec((seq, hps * MLA_NOPE_DIM), lambda b, h, qi: (b, h)),
                  pl.BlockSpec((seq, LANES), lambda b, h, qi: (b, 0)),
                  pl.BlockSpec((seq, hps * MLA_V_DIM), lambda b, h, qi: (b, vcol + h))],
        out_specs=pl.BlockSpec((blk, hps * MLA_V_DIM), lambda b, h, qi: (b * nq + qi, h)),
        scratch_shapes=[pltpu.VMEM((hps, blk, LANES), F32),
                        pltpu.VMEM((hps, blk, 2 * LANES), F32)],
        compiler_params=_cparams(3),
        name="mla_attn",
    )(q_cat, kv, kr2, kv)


def _sigmoid(x):
    return 1.0 / (1.0 + jnp.exp(-x))


def _gated_kernel(oa_ref, ob_ref, wa_ref, wb_ref, ga_ref, gb_ref, o_ref):
    ya = jnp.dot(oa_ref[...], wa_ref[...].astype(BF16), preferred_element_type=F32)
    yb = jnp.dot(ob_ref[...], wb_ref[...].astype(BF16), preferred_element_type=F32)
    y = _sigmoid(ga_ref[...].astype(F32)) * ya + _sigmoid(gb_ref[...].astype(F32)) * yb
    o_ref[...] = y.astype(o_ref.dtype)


def _gated(o_a, o_b, w_a, w_b, h_g, l, tm, tn):
    m = o_a.shape[0]
    d = w_a.shape[2]
    tm = min(tm, m)
    ngb = d // tn
    return pl.pallas_call(
        _gated_kernel,
        out_shape=SDS((m, d), BF16),
        grid=(m // tm, ngb),
        in_specs=[pl.BlockSpec((tm, DA_V_W), lambda i, j: (i, 0)),
                  pl.BlockSpec((tm, MLA_O_W), lambda i, j: (i, 0)),
                  pl.BlockSpec((None, DA_V_W, tn), lambda i, j: (l, 0, j)),
                  pl.BlockSpec((None, MLA_O_W, tn), lambda i, j: (l, 0, j)),
                  pl.BlockSpec((tm, tn), lambda i, j: (i, j)),
                  pl.BlockSpec((tm, tn), lambda i, j: (i, ngb + j))],
        out_specs=pl.BlockSpec((tm, tn), lambda i, j: (i, j)),
        compiler_params=_cparams(2),
        name="gated",
    )(o_a, o_b, w_a, w_b, h_g, h_g)


def _layer_norm(z, g, b):
    mu = jnp.mean(z, axis=-1, keepdims=True)
    zc = z - mu
    var = jnp.mean(zc * zc, axis=-1, keepdims=True)
    return zc * lax.rsqrt(var + NORM_EPS) * g + b


def _split_bf16(x):
    hi = x.astype(BF16)
    lo = (x - hi.astype(F32)).astype(BF16)
    return hi, lo


def _route_record(lg):
    lane = lax.broadcasted_iota(I32, lg.shape, 1)
    lane_f = lane.astype(F32)
    big = float(ROUTE_W)
    gmask = lane < N_GROUPS
    g_max = jnp.max(jnp.where(gmask, lg, NEG_INF), axis=1, keepdims=True)
    g_idx = jnp.min(jnp.where(gmask & (lg == g_max), lane_f, big), axis=1, keepdims=True)
    g_w = 1.0 / jnp.sum(jnp.where(gmask, jnp.exp(lg - g_max), 0.0), axis=1, keepdims=True)

    lo = N_GROUPS + EXPERTS_PER_GROUP * g_idx
    sel = (lane_f >= lo) & (lane_f < lo + EXPERTS_PER_GROUP)
    e_max = jnp.max(jnp.where(sel, lg, NEG_INF), axis=1, keepdims=True)
    pe = jnp.where(sel, jnp.exp(lg - e_max), 0.0)
    prob = pe / jnp.sum(pe, axis=1, keepdims=True)
    p1 = jnp.max(jnp.where(sel, prob, -1.0), axis=1, keepdims=True)
    i1 = jnp.min(jnp.where(sel & (prob == p1), lane_f, big), axis=1, keepdims=True)
    sel2 = sel & (lane_f != i1)
    p2 = jnp.max(jnp.where(sel2, prob, -1.0), axis=1, keepdims=True)
    i2 = jnp.min(jnp.where(sel2 & (prob == p2), lane_f, big), axis=1, keepdims=True)
    tot = p1 + p2
    w1 = g_w * (p1 / tot)
    w2 = g_w * (p2 / tot)
    rec = jnp.where(lane == 0, i1 - N_GROUPS,
          jnp.where(lane == 1, i2 - N_GROUPS,
          jnp.where(lane == 2, w1, jnp.where(lane == 3, w2, 0.0))))
    return rec


def _outproj_kernel(y_ref, w_ref, x_ref, g_ref, b_ref, wr_ref, br_ref,
                    xo_ref, xt_ref, route_ref, *, alpha):
    mix = jnp.dot(y_ref[...], w_ref[...].astype(BF16), preferred_element_type=F32)
    xn = _layer_norm(alpha * x_ref[...] + mix, g_ref[...], b_ref[...])
    xo_ref[...] = xn
    _wide_to_tiled(xn, xt_ref)
    xh, xl = _split_bf16(xn)
    wh, wl = _split_bf16(wr_ref[...])
    lg = (jnp.dot(xh, wh, preferred_element_type=F32) + jnp.dot(xl, wh, preferred_element_type=F32)
          + jnp.dot(xh, wl, preferred_element_type=F32) + br_ref[...])
    route_ref[...] = _route_record(lg)


def _outproj_ln_route(y, w_out, x, ln_g, ln_b, w_r, b_r, l, alpha, tm):
    m, d = x.shape
    tm = min(tm, m)
    const = dict(pipeline_mode=pl.Buffered(1))
    return pl.pallas_call(
        functools.partial(_outproj_kernel, alpha=alpha),
        out_shape=(SDS((m, d), F32), SDS((m * ROW_TILES, LANES), U32), SDS((m, ROUTE_W), F32)),
        grid=(m // tm,),
        in_specs=[pl.BlockSpec((tm, d), lambda i: (i, 0)),
                  pl.BlockSpec((None, d, d), lambda i: (l, 0, 0), **const),
                  pl.BlockSpec((tm, d), lambda i: (i, 0)),
                  pl.BlockSpec((None, 1, d), lambda i: (l, 0, 0)),
                  pl.BlockSpec((None, 1, d), lambda i: (l, 0, 0)),
                  pl.BlockSpec((None, d, ROUTE_W), lambda i: (0, 0, 0)),
                  pl.BlockSpec((None, 1, ROUTE_W), lambda i: (0, 0, 0))],
        out_specs=(pl.BlockSpec((tm, d), lambda i: (i, 0)),
                   pl.BlockSpec((tm * ROW_TILES, LANES), lambda i: (i, 0)),
                   pl.BlockSpec((tm, ROUTE_W), lambda i: (i, 0))),
        compiler_params=_cparams(1),
        name="outproj_ln_route",
    )(y, w_out, x, ln_g, ln_b, w_r, b_r)


def _row_copy(src_hbm, row, dst, r, sem):
    src = src_hbm.at[pl.ds(pl.multiple_of(row * ROW_TILES, ROW_TILES), ROW_TILES), :]
    return pltpu.make_async_copy(src, dst.at[pl.ds(pl.multiple_of(r * ROW_TILES, ROW_TILES), ROW_TILES), :], sem)


HI_MASK = 0xFFFF0000


def _tiled_to_wide(ref, n):
    words = [ref[pl.ds(s, n, stride=ROW_TILES), :] for s in range(ROW_TILES)]
    lo = [pltpu.bitcast(w << 16, F32) for w in words]
    hi = [pltpu.bitcast(w & jnp.uint32(HI_MASK), F32) for w in words]
    return jnp.concatenate(lo + hi, axis=1)


def _wide_to_tiled(val, ref):
    n = val.shape[0]
    bits = pltpu.bitcast(val.astype(BF16).astype(F32), U32)
    for s in range(ROW_TILES):
        lo = bits[:, s * LANES:(s + 1) * LANES] >> 16
        hi = bits[:, (ROW_TILES + s) * LANES:(ROW_TILES + s + 1) * LANES]
        ref[pl.ds(s, n, stride=ROW_TILES), :] = hi | lo


def _gather_rows(src_hbm, idx_ref, base, dst, sem, n):
    assert n % GATHER_UNROLL == 0

    def issue(i, c):
        for u in range(GATHER_UNROLL):
            r = i * GATHER_UNROLL + u
            _row_copy(src_hbm, idx_ref[base + r], dst, r, sem).start(priority=u % DMA_QUEUES)
        return c

    lax.fori_loop(0, n // GATHER_UNROLL, issue, 0)


def _gather_wait(src_hbm, dst, sem, n):
    def wait(r, c):
        _row_copy(src_hbm, 0, dst, r, sem).wait()
        return c

    lax.fori_loop(0, n, wait, 0, unroll=GATHER_UNROLL)


def _moe_kernel(bs_ref, nb_ref, nu_ref, tok_ref, x_hbm, wg_ref, wu_ref, wd_ref, y_hbm,
                xbuf, obuf, gsem, osem, *, bm, n_blocks):
    e = pl.program_id(0)
    n_used = nu_ref[0]

    def out_copy(g, slot):
        rows = pl.ds(pl.multiple_of(g * (bm * ROW_TILES), bm * ROW_TILES), bm * ROW_TILES)
        return pltpu.make_async_copy(obuf.at[slot], y_hbm.at[rows, :], osem.at[slot])

    @pl.when(e == 0)
    def _():
        _gather_rows(x_hbm, tok_ref, 0, xbuf.at[0], gsem.at[0], bm)

    def body(j, c):
        g = bs_ref[e] + j
        slot = lax.rem(g, 2)

        _gather_wait(x_hbm, xbuf.at[slot], gsem.at[slot], bm)

        @pl.when(g >= 2)
        def _():
            out_copy(g - 2, slot).wait()

        g_next = jnp.minimum(g + 1, n_blocks - 1)
        for r in range(bm):
            _row_copy(x_hbm, tok_ref[g_next * bm + r], xbuf.at[1 - slot], r,
                      gsem.at[1 - slot]).start(priority=r % DMA_QUEUES)

        xb = _tiled_to_wide(xbuf.at[slot], bm).astype(BF16)
        gt = _dot_nt(xb, wg_ref[...].astype(BF16))
        up = _dot_nt(xb, wu_ref[...].astype(BF16))
        hid = ((gt * _sigmoid(gt)) * up).astype(BF16)
        _wide_to_tiled(jnp.dot(hid, wd_ref[...].astype(BF16), preferred_element_type=F32),
                       obuf.at[slot])
        out_copy(g, slot).start()
        return c

    lax.fori_loop(0, nb_ref[e], body, 0)

    @pl.when(e == pl.num_programs(0) - 1)
    def _():
        @pl.when(n_used >= 2)
        def _():
            out_copy(n_used - 2, lax.rem(n_used - 2, 2)).wait()

        out_copy(n_used - 1, lax.rem(n_used - 1, 2)).wait()
        _gather_wait(x_hbm, xbuf.at[lax.rem(n_used, 2)], gsem.at[lax.rem(n_used, 2)], bm)

        obuf[0] = jnp.zeros(obuf.shape[1:], obuf.dtype)

        def fill(g, c):
            cp = out_copy(g, 0)
            cp.start()
            cp.wait()
            return c

        lax.fori_loop(n_used, n_blocks, fill, 0)


def _moe_ffn(x_tiled, bstart, nblk, n_used, row_tok, wgt, wut, wd, l, bm):
    f, d = wd.shape[2], wd.shape[3]
    assert d == 2 * ROW_TILES * LANES
    n_blocks = row_tok.shape[0] // bm
    w_spec = pl.BlockSpec((None, None, f, d), lambda e, bs, nb, nu, tok: (l, e, 0, 0))
    grid_spec = pltpu.PrefetchScalarGridSpec(
        num_scalar_prefetch=4,
        grid=(N_EXPERTS,),
        in_specs=[pl.BlockSpec(memory_space=pl.ANY), w_spec, w_spec, w_spec],
        out_specs=pl.BlockSpec(memory_space=pl.ANY),
        scratch_shapes=[pltpu.VMEM((2, bm * ROW_TILES, LANES), U32),
                        pltpu.VMEM((2, bm * ROW_TILES, LANES), U32),
                        pltpu.SemaphoreType.DMA((2,)), pltpu.SemaphoreType.DMA((2,))],
    )
    return pl.pallas_call(
        functools.partial(_moe_kernel, bm=bm, n_blocks=n_blocks),
        out_shape=SDS((n_blocks * bm * ROW_TILES, LANES), U32),
        grid_spec=grid_spec,
        compiler_params=_cparams(1),
        name="moe_ffn",
    )(bstart, nblk, n_used, row_tok, x_tiled, wgt, wut, wd)


def _combine_kernel(pos_ref, yb_hbm, x_ref, route_ref, g_ref, b_ref, xo_ref, xbo_ref,
                    buf, sems, *, tm, t_total, alpha):
    i = pl.program_id(0)
    slot = lax.rem(i, 2)

    def gather(step, s):
        for k in range(TOP_K):
            _gather_rows(yb_hbm, pos_ref, k * t_total + step * tm, buf.at[s, k],
                         sems.at[s * TOP_K + k], tm)

    @pl.when(i == 0)
    def _():
        gather(0, 0)

    @pl.when(i + 1 < pl.num_programs(0))
    def _():
        gather(i + 1, 1 - slot)

    for k in range(TOP_K):
        _gather_wait(yb_hbm, buf.at[slot, k], sems.at[slot * TOP_K + k], tm)

    rec = route_ref[...]
    z = (alpha * x_ref[...] + rec[:, 2:3] * _tiled_to_wide(buf.at[slot, 0], tm)
         + rec[:, 3:4] * _tiled_to_wide(buf.at[slot, 1], tm))
    xn = _layer_norm(z, g_ref[...], b_ref[...])
    xo_ref[...] = xn
    xbo_ref[...] = xn.astype(BF16)


def _combine_ln(pos_k, yb, x, route, ln_g, ln_b, l, alpha, tm):
    t, d = x.shape
    tm = min(tm, t)
    grid_spec = pltpu.PrefetchScalarGridSpec(
        num_scalar_prefetch=1,
        grid=(t // tm,),
        in_specs=[pl.BlockSpec(memory_space=pl.ANY),
                  pl.BlockSpec((tm, d), lambda i, pos: (i, 0)),
                  pl.BlockSpec((tm, ROUTE_W), lambda i, pos: (i, 0)),
                  pl.BlockSpec((None, 1, d), lambda i, pos: (l, 0, 0)),
                  pl.BlockSpec((None, 1, d), lambda i, pos: (l, 0, 0))],
        out_specs=(pl.BlockSpec((tm, d), lambda i, pos: (i, 0)),
                   pl.BlockSpec((tm, d), lambda i, pos: (i, 0))),
        scratch_shapes=[pltpu.VMEM((2, TOP_K, tm * ROW_TILES, LANES), U32),
                        pltpu.SemaphoreType.DMA((2 * TOP_K,))],
    )
    return pl.pallas_call(
        functools.partial(_combine_kernel, tm=tm, t_total=t, alpha=alpha),
        out_shape=(SDS((t, d), F32), SDS((t, d), BF16)),
        grid_spec=grid_spec,
        compiler_params=_cparams(1),
        name="combine_ln",
    )(pos_k, yb, x, route, ln_g, ln_b)


def _route_meta(e_ids, bm):
    t = e_ids.shape[0]
    a = t * TOP_K
    flat_e = e_ids.reshape(a)
    onehot = (flat_e[:, None] == jnp.arange(N_EXPERTS, dtype=I32)[None, :]).astype(I32)
    csum = jnp.cumsum(onehot, axis=0)
    rank = jnp.take_along_axis(csum, flat_e[:, None], axis=1)[:, 0] - 1
    counts = csum[-1]
    nblk = (counts + bm - 1) // bm
    bend = jnp.cumsum(nblk)
    bstart = bend - nblk
    dest = (bstart[flat_e] * bm + rank).astype(I32)
    n_blocks = a // bm + N_EXPERTS
    row_tok = jnp.zeros((n_blocks * bm,), I32).at[dest].set(jnp.arange(a, dtype=I32) // TOP_K)
    pos_k = dest.reshape(t, TOP_K).T.reshape(a)
    return pos_k, row_tok, bstart.astype(I32), nblk.astype(I32), bend[-1:].astype(I32)


def _swap_halves(w, axis):
    half = w.shape[axis] // 2
    lo = lax.slice_in_dim(w, 0, half, axis=axis)
    hi = lax.slice_in_dim(w, half, 2 * half, axis=axis)
    return jnp.concatenate([-hi, lo], axis=axis)


def _prep_layer_weights(wk_t, w_uq_l, w_ukv_l, w_rg_l, b_rg_l, w_re_l, b_re_l):
    d = wk_t.shape[1]
    w_krt = jnp.concatenate([wk_t, _swap_halves(wk_t, 0)], axis=0)
    wq = w_uq_l.reshape(MLA_Q_RANK, MLA_HEADS, MLA_NOPE_DIM + MLA_ROPE_DIM)
    wq_rope = wq[..., MLA_NOPE_DIM:]
    w_uq2 = jnp.concatenate([wq[..., :MLA_NOPE_DIM], wq_rope, _swap_halves(wq_rope, 2)],
                            axis=-1).reshape(MLA_Q_RANK, MLA_HEADS * MLA_QCAT)
    wkv = w_ukv_l.reshape(MLA_KV_RANK, MLA_HEADS, MLA_NOPE_DIM + MLA_V_DIM)
    w_ukv2 = jnp.concatenate([wkv[..., :MLA_NOPE_DIM].reshape(MLA_KV_RANK, -1),
                              wkv[..., MLA_NOPE_DIM:].reshape(MLA_KV_RANK, -1)], axis=1)
    pad = ROUTE_W - N_GROUPS - N_EXPERTS
    w_r = jnp.concatenate([w_rg_l, w_re_l, jnp.zeros((d, pad), F32)], axis=1)[None]
    b_r = jnp.concatenate([b_rg_l, b_re_l, jnp.zeros((pad,), F32)])[None, None]
    return w_krt, w_uq2, w_ukv2, w_r, b_r


def _rope_tables(seq):
    inv = ROPE_THETA ** (-jnp.arange(0, MLA_ROPE_DIM, 2, dtype=F32) / MLA_ROPE_DIM)
    ang = jnp.arange(seq, dtype=F32)[:, None] * inv[None, :]
    return jnp.cos(ang), jnp.sin(ang)


def kernel(x, w_in, da_lambda, da_subln_g, mla_q_norm_g, mla_w_uq, mla_kv_norm_g, mla_w_ukv, w_branch_a, w_branch_b, w_out, rel_bias, ln1_g, ln1_b, router_w_group, router_b_group, router_w_expert, router_b_expert, expert_w_gate, expert_w_up, expert_w_down, ln2_g, ln2_b):
    batch, seq, d = x.shape
    depth = w_in.shape[0]
    t = batch * seq
    alpha = (2 * depth) ** 0.25
    main_w = DA_QK_W * 2 + DA_V_W + MLA_Q_RANK + MLA_KV_RANK
    gate0 = main_w + MLA_ROPE_DIM

    cos, sin = _rope_tables(seq)
    tabk = jnp.concatenate([cos, cos, sin, sin], axis=1)
    q_scale = (MLA_NOPE_DIM + MLA_ROPE_DIM) ** -0.5
    tabq = q_scale * jnp.concatenate([jnp.ones((seq, MLA_NOPE_DIM), F32), tabk], axis=1)

    subln_g = da_subln_g[:, None, :]
    ln1_g3, ln1_b3 = ln1_g[:, None, :], ln1_b[:, None, :]
    ln2_g3, ln2_b3 = ln2_g[:, None, :], ln2_b[:, None, :]

    wt_in = jnp.swapaxes(w_in, 1, 2)
    wgt = jnp.swapaxes(expert_w_gate, 2, 3)
    wut = jnp.swapaxes(expert_w_up, 2, 3)

    xf = x.reshape(t, d)
    xb = xf.astype(BF16)
    for l in range(depth):
        lam_init = 0.8 - 0.6 * math.exp(-0.3 * l)
        w_krt, w_uq2, w_ukv2, w_r, b_r = _prep_layer_weights(
            wt_in[l, main_w:gate0, :], mla_w_uq[l], mla_w_ukv[l], router_w_group[l],
            router_b_group[l], router_w_expert[l], router_b_expert[l])

        h_main = _matmul_nt(xb, wt_in, l, 0, main_w, 4096, 256, BF16, "mm_main")
        h_g = _matmul_nt(xb, wt_in, l, gate0, 2 * d, 2048, 512, BF16, "mm_gates")
        kr2 = _krope(xb, w_krt, tabk, seq, 512)
        o_a = _da_attention(h_main, rel_bias, da_lambda, subln_g, l, lam_init, batch, seq)
        q_cat = _uq(h_main, mla_q_norm_g[l][None], w_uq2, tabq, seq, 512)
        kv = _ukv(h_main, mla_kv_norm_g[l][None], w_ukv2, 512)
        o_b = _mla_attention(q_cat, kv, kr2, batch, seq)
        y = _gated(o_a, o_b, w_branch_a, w_branch_b, h_g, l, 1024, 512)
        x1, x1_tiled, route = _outproj_ln_route(y, w_out, xf, ln1_g3, ln1_b3, w_r, b_r, l, alpha, 256)

        e_ids = route[:, :TOP_K].astype(I32)
        pos_k, row_tok, bstart, nblk, n_used = _route_meta(e_ids, MOE_BLOCK)
        yb = _moe_ffn(x1_tiled, bstart, nblk, n_used, row_tok, wgt, wut, expert_w_down, l, MOE_BLOCK)
        xf, xb = _combine_ln(pos_k, yb, x1, route, ln2_g3, ln2_b3, l, alpha, 256)
    return xf.reshape(batch, seq, d)
```

```python
import functools
import math

import jax
import jax.numpy as jnp
from jax import lax
from jax.experimental import pallas as pl
from jax.experimental.pallas import tpu as pltpu

F32 = jnp.float32
BF16 = jnp.bfloat16
I32 = jnp.int32
U32 = jnp.uint32
SDS = jax.ShapeDtypeStruct

DA_HEADS = 8
DA_HEAD_DIM = 64
DA_V_DIM = 2 * DA_HEAD_DIM
MLA_HEADS = 8
MLA_Q_RANK = 768
MLA_KV_RANK = 512
MLA_NOPE_DIM = 128
MLA_ROPE_DIM = 64
MLA_V_DIM = 128
ROPE_THETA = 10000.0
REL_BUCKETS = 32
REL_MAX_DIST = 128
REL_MAX_EXACT = REL_BUCKETS // 2
N_GROUPS = 4
EXPERTS_PER_GROUP = 8
N_EXPERTS = N_GROUPS * EXPERTS_PER_GROUP
TOP_K = 2
NORM_EPS = 1e-5
NEG_INF = -1e30

DA_QK_W = DA_HEADS * 2 * DA_HEAD_DIM
DA_V_W = DA_HEADS * DA_V_DIM
MLA_O_W = MLA_HEADS * MLA_V_DIM
MLA_QCAT = 2 * MLA_NOPE_DIM

LANES = 128
SUBLANES = 8
VMEM_LIMIT = 56 * 1024 * 1024

ATT_K_BLOCK = 512
DA_Q_BLOCK = 256
MOE_BLOCK = 256
ROUTE_W = LANES
GATHER_UNROLL = 8
ROW_TILES = 8
HEADS_PER_STEP = 4
DMA_QUEUES = 2
GATHER_AHEAD = 3
GATHER_SLOTS = GATHER_AHEAD + 1


def _cparams(n_axes):
    return pltpu.CompilerParams(dimension_semantics=("arbitrary",) * n_axes,
                                vmem_limit_bytes=VMEM_LIMIT)


def _dot_nt(a, b):
    return lax.dot_general(a, b, (((1,), (1,)), ((), ())), preferred_element_type=F32)


def _mm_nt_kernel(a_ref, w_ref, o_ref):
    o_ref[...] = _dot_nt(a_ref[...], w_ref[0].astype(BF16)).astype(o_ref.dtype)


def _matmul_nt(a, wt3, l, row0, n_rows, tm, tn, out_dtype, name):
    m, k = a.shape
    tm = min(tm, m)
    assert m % tm == 0 and n_rows % tn == 0
    if row0 % tn == 0:
        r0 = row0 // tn
        w_spec = pl.BlockSpec((1, tn, k), lambda i, j: (l, r0 + j, 0))
    else:
        w_spec = pl.BlockSpec((pl.Element(1), pl.Element(tn), pl.Element(k)),
                              lambda i, j: (l, pl.multiple_of(row0 + j * tn, SUBLANES), 0))
    return pl.pallas_call(
        _mm_nt_kernel,
        out_shape=SDS((m, n_rows), out_dtype),
        grid=(m // tm, n_rows // tn),
        in_specs=[pl.BlockSpec((tm, k), lambda i, j: (i, 0)), w_spec],
        out_specs=pl.BlockSpec((tm, tn), lambda i, j: (i, j)),
        compiler_params=_cparams(2),
        name=name,
    )(a, wt3)


def _krope_kernel(a_ref, w_ref, tab_ref, o_ref):
    t = _dot_nt(a_ref[...], w_ref[...].astype(BF16))
    t = t * tab_ref[...]
    o_ref[...] = (t + pltpu.roll(t, MLA_ROPE_DIM, axis=1)).astype(o_ref.dtype)


def _krope(xb, w_kr, tabk, seq, tm):
    m, k = xb.shape
    tm = min(tm, seq)
    nsb = seq // tm
    return pl.pallas_call(
        _krope_kernel,
        out_shape=SDS((m, LANES), BF16),
        grid=(m // tm,),
        in_specs=[pl.BlockSpec((tm, k), lambda i: (i, 0)),
                  pl.BlockSpec((LANES, k), lambda i: (0, 0)),
                  pl.BlockSpec((tm, LANES), lambda i: (i % nsb, 0))],
        out_specs=pl.BlockSpec((tm, LANES), lambda i: (i, 0)),
        compiler_params=_cparams(1),
        name="krope",
    )(xb, w_kr, tabk)


def _rms(c, g):
    return c * lax.rsqrt(jnp.mean(c * c, axis=-1, keepdims=True) + NORM_EPS) * g


def _uq_kernel(c_ref, g_ref, w_ref, tab_ref, o_ref):
    n = _rms(c_ref[...].astype(F32), g_ref[...])
    acc = jnp.dot(n.astype(BF16), w_ref[...].astype(BF16), preferred_element_type=F32)
    tab = tab_ref[...]
    for h in range(MLA_HEADS):
        sl = slice(h * MLA_QCAT, (h + 1) * MLA_QCAT)
        o_ref[:, sl] = (acc[:, sl] * tab).astype(o_ref.dtype)


def _uq(h_main, g, w_uq2, tabq, seq, tm):
    m = h_main.shape[0]
    tm = min(tm, seq)
    nsb = seq // tm
    cq_blk = (DA_QK_W * 2 + DA_V_W) // MLA_Q_RANK
    n_out = MLA_HEADS * MLA_QCAT
    return pl.pallas_call(
        _uq_kernel,
        out_shape=SDS((m, n_out), BF16),
        grid=(m // tm,),
        in_specs=[pl.BlockSpec((tm, MLA_Q_RANK), lambda i: (i, cq_blk)),
                  pl.BlockSpec((1, MLA_Q_RANK), lambda i: (0, 0)),
                  pl.BlockSpec((MLA_Q_RANK, n_out), lambda i: (0, 0)),
                  pl.BlockSpec((tm, MLA_QCAT), lambda i: (i % nsb, 0))],
        out_specs=pl.BlockSpec((tm, n_out), lambda i: (i, 0)),
        compiler_params=_cparams(1),
        name="mla_uq",
    )(h_main, g, w_uq2, tabq)


def _ukv_kernel(c0_ref, c1_ref, g_ref, w_ref, o_ref):
    c = jnp.concatenate([c0_ref[...], c1_ref[...]], axis=1).astype(F32)
    n = _rms(c, g_ref[...])
    o_ref[...] = jnp.dot(n.astype(BF16), w_ref[...].astype(BF16),
                         preferred_element_type=F32).astype(o_ref.dtype)


def _ukv(h_main, g, w_ukv2, tm):
    m = h_main.shape[0]
    tm = min(tm, m)
    half = MLA_KV_RANK // 2
    b0 = (DA_QK_W * 2 + DA_V_W + MLA_Q_RANK) // half
    n_out = w_ukv2.shape[1]
    return pl.pallas_call(
        _ukv_kernel,
        out_shape=SDS((m, n_out), BF16),
        grid=(m // tm,),
        in_specs=[pl.BlockSpec((tm, half), lambda i: (i, b0)),
                  pl.BlockSpec((tm, half), lambda i: (i, b0 + 1)),
                  pl.BlockSpec((1, MLA_KV_RANK), lambda i: (0, 0)),
                  pl.BlockSpec((MLA_KV_RANK, n_out), lambda i: (0, 0))],
        out_specs=pl.BlockSpec((tm, n_out), lambda i: (i, 0)),
        compiler_params=_cparams(1),
        name="mla_ukv",
    )(h_main, h_main, g, w_ukv2)


def _flash_init(m_scr, acc_scr):
    m_scr[...] = jnp.full(m_scr.shape, NEG_INF, F32)
    acc_scr[...] = jnp.zeros(acc_scr.shape, F32)


def _flash_update(s, v, m_scr, acc_scr):
    v1 = jnp.concatenate([v, jnp.ones(v.shape, v.dtype)], axis=1)
    m_prev = m_scr[...]
    m_new = jnp.maximum(m_prev, jnp.max(s, axis=1, keepdims=True))
    p = jnp.exp(s - jnp.concatenate([m_new] * (s.shape[1] // LANES), axis=1))
    alpha = jnp.exp(m_prev - m_new)
    acc_scr[...] = (jnp.concatenate([alpha, alpha], axis=1) * acc_scr[...]
                    + jnp.dot(p.astype(BF16), v1, preferred_element_type=F32))
    m_scr[...] = m_new


def _flash_result(acc_scr):
    acc = acc_scr[...]
    return acc[:, :LANES] / acc[:, LANES:]


def _t5_bias_blocks(rb_ref, h, bias_scr, tq, tk):
    row = lax.broadcasted_iota(I32, (tq, tk), 0)
    col = lax.broadcasted_iota(I32, (tq, tk), 1)
    far = rb_ref[REL_BUCKETS - 1, h]
    for d in range(bias_scr.shape[0]):
        n = row - col + d * tq
        nn = jnp.maximum(n, 0)
        nf = jnp.maximum(nn, 1).astype(F32)
        large = REL_MAX_EXACT + (jnp.log(nf / REL_MAX_EXACT) / math.log(REL_MAX_DIST / REL_MAX_EXACT)
                                 * (REL_BUCKETS - REL_MAX_EXACT)).astype(I32)
        large = jnp.minimum(large, REL_BUCKETS - 1)
        bucket = jnp.where(nn < REL_MAX_EXACT, nn, large)
        val = jnp.zeros((tq, tk), F32)
        for bb in range(REL_BUCKETS):
            val = jnp.where(bucket == bb, rb_ref[bb, h], val)
        bias_scr[d] = jnp.where(n >= 0, val - far, NEG_INF)


def _da_kernel(rb_ref, lamv_ref, q_ref, k_ref, v_ref, g_ref, o_ref,
               bias_scr, m_scr, acc_scr, *, tq, tk, lam_init):
    hp = pl.program_id(0)
    b = pl.program_id(1)
    qi = pl.program_id(2)
    ratio = tk // tq
    jd = qi // ratio
    par = qi % ratio
    hd = 2 * DA_HEAD_DIM
    heads = range(HEADS_PER_STEP)

    @pl.when((b == 0) & (qi == 0))
    def _():
        for hh in heads:
            _t5_bias_blocks(rb_ref, hp * HEADS_PER_STEP + hh, bias_scr.at[hh], tq, tk)

    scale = DA_HEAD_DIM ** -0.5
    q_all = q_ref[...]
    q2 = []
    for hh in heads:
        q = q_all[:, hh * hd:(hh + 1) * hd]
        lane = lax.broadcasted_iota(I32, q.shape, 1)
        zero = jnp.zeros_like(q)
        q2.append(jnp.concatenate([jnp.where(lane < DA_HEAD_DIM, q, zero),
                                   jnp.where(lane >= DA_HEAD_DIM, q, zero)], axis=0) * scale)
        _flash_init(m_scr.at[hh], acc_scr.at[hh])

    def block(j, bias_idx):
        off = pl.multiple_of(j * tk, tk)
        kb = k_ref[pl.ds(off, tk), :]
        vb = v_ref[pl.ds(off, tk), :]
        for hh in heads:
            s = _dot_nt(q2[hh], kb[:, hh * hd:(hh + 1) * hd])
            if bias_idx is not None:
                bias = bias_scr[hh, bias_idx]
                s = s + jnp.concatenate([bias, bias], axis=0)
            _flash_update(s, vb[:, hh * DA_V_DIM:(hh + 1) * DA_V_DIM], m_scr.at[hh], acc_scr.at[hh])

    def far_body(j, c):
        block(j, None)
        return c

    lax.fori_loop(0, jd - 1, far_body, 0)

    @pl.when((jd >= 1) & (par == 0))
    def _():
        block(jd - 1, ratio)

    @pl.when((jd >= 1) & (par != 0))
    def _():
        block(jd - 1, None)

    block(jd, par)

    lamv = lamv_ref[...]
    lam = (jnp.exp(jnp.sum(lamv[0:1] * lamv[1:2], axis=1, keepdims=True))
           - jnp.exp(jnp.sum(lamv[2:3] * lamv[3:4], axis=1, keepdims=True)) + lam_init)
    for hh in heads:
        o12 = _flash_result(acc_scr.at[hh])
        o = o12[:tq] - lam * o12[tq:]
        o = o * lax.rsqrt(jnp.mean(o * o, axis=1, keepdims=True) + NORM_EPS) * g_ref[...] * (1.0 - lam_init)
        o_ref[:, hh * DA_V_DIM:(hh + 1) * DA_V_DIM] = o.astype(o_ref.dtype)


def _da_attention(h_main, rel_bias, da_lambda, subln_g, l, lam_init, batch, seq):
    tq, tk = DA_Q_BLOCK, ATT_K_BLOCK
    ratio = tk // tq
    assert seq % tk == 0 and tk % tq == 0 and tq + 1 >= REL_MAX_DIST
    nq = seq // tq
    hw = HEADS_PER_STEP * 2 * DA_HEAD_DIM
    vw = HEADS_PER_STEP * DA_V_DIM
    kcol = DA_QK_W // hw
    vcol = 2 * DA_QK_W // vw
    return pl.pallas_call(
        functools.partial(_da_kernel, tq=tq, tk=tk, lam_init=lam_init),
        out_shape=SDS((batch * seq, DA_V_W), BF16),
        grid=(DA_HEADS // HEADS_PER_STEP, batch, nq),
        in_specs=[pl.BlockSpec(memory_space=pltpu.SMEM),
                  pl.BlockSpec((None, 4, DA_HEAD_DIM), lambda h, b, qi: (l, 0, 0)),
                  pl.BlockSpec((tq, hw), lambda h, b, qi: (b * nq + qi, h)),
                  pl.BlockSpec((seq, hw), lambda h, b, qi: (b, kcol + h)),
                  pl.BlockSpec((seq, vw), lambda h, b, qi: (b, vcol + h)),
                  pl.BlockSpec((None, 1, DA_V_DIM), lambda h, b, qi: (l, 0, 0))],
        out_specs=pl.BlockSpec((tq, vw), lambda h, b, qi: (b * nq + qi, h)),
        scratch_shapes=[pltpu.VMEM((HEADS_PER_STEP, ratio + 1, tq, tk), F32),
                        pltpu.VMEM((HEADS_PER_STEP, 2 * tq, LANES), F32),
                        pltpu.VMEM((HEADS_PER_STEP, 2 * tq, 2 * LANES), F32)],
        compiler_params=_cparams(3),
        name="da_attn",
    )(rel_bias, da_lambda, h_main, h_main, h_main, subln_g)


def _mla_kernel(q_ref, kn_ref, kr_ref, v_ref, o_ref, m_scr, acc_scr, *, blk):
    qi = pl.program_id(2)
    heads = range(HEADS_PER_STEP)
    q_all = q_ref[...]
    q = [q_all[:, hh * MLA_QCAT:(hh + 1) * MLA_QCAT] for hh in heads]
    for hh in heads:
        _flash_init(m_scr.at[hh], acc_scr.at[hh])

    def block(j, masked):
        off = pl.multiple_of(j * blk, blk)
        kn = kn_ref[pl.ds(off, blk), :]
        kr = kr_ref[pl.ds(off, blk), :]
        vb = v_ref[pl.ds(off, blk), :]
        for hh in heads:
            kcat = jnp.concatenate([kn[:, hh * MLA_NOPE_DIM:(hh + 1) * MLA_NOPE_DIM], kr], axis=1)
            s = _dot_nt(q[hh], kcat)
            if masked:
                row = lax.broadcasted_iota(I32, s.shape, 0)
                col = lax.broadcasted_iota(I32, s.shape, 1)
                s = jnp.where(col <= row, s, NEG_INF)
            _flash_update(s, vb[:, hh * MLA_V_DIM:(hh + 1) * MLA_V_DIM], m_scr.at[hh], acc_scr.at[hh])

    def far_body(j, c):
        block(j, False)
        return c

    lax.fori_loop(0, qi, far_body, 0)
    block(qi, True)
    for hh in heads:
        o_ref[:, hh * MLA_V_DIM:(hh + 1) * MLA_V_DIM] = _flash_result(acc_scr.at[hh]).astype(o_ref.dtype)


def _mla_attention(q_cat, kv, kr2, batch, seq):
    blk = min(ATT_K_BLOCK, seq)
    nq = seq // blk
    hps = HEADS_PER_STEP
    vcol = MLA_HEADS // hps
    return pl.pallas_call(
        functools.partial(_mla_kernel, blk=blk),
        out_shape=SDS((batch * seq, MLA_O_W), BF16),
        grid=(batch, MLA_HEADS // hps, nq),
        in_specs=[pl.BlockSpec((blk, hps * MLA_QCAT), lambda b, h, qi: (b * nq + qi, h)),
                  pl.BlockSpec((seq, hps * MLA_NOPE_DIM), lambda b, h, qi: (b, h)),
                  pl.BlockSpec((seq, LANES), lambda b, h, qi: (b, 0)),
                  pl.BlockSpec((seq, hps * MLA_V_DIM), lambda b, h, qi: (b, vcol + h))],
        out_specs=pl.BlockSpec((blk, hps * MLA_V_DIM), lambda b, h, qi: (b * nq + qi, h)),
        scratch_shapes=[pltpu.VMEM((hps, blk, LANES), F32),
                        pltpu.VMEM((hps, blk, 2 * LANES), F32)],
        compiler_params=_cparams(3),
        name="mla_attn",
    )(q_cat, kv, kr2, kv)


def _sigmoid(x):
    return 1.0 / (1.0 + jnp.exp(-x))


def _gated_kernel(oa_ref, ob_ref, wa_ref, wb_ref, ga_ref, gb_ref, o_ref):
    ya = jnp.dot(oa_ref[...], wa_ref[...].astype(BF16), preferred_element_type=F32)
    yb = jnp.dot(ob_ref[...], wb_ref[...].astype(BF16), preferred_element_type=F32)
    y = _sigmoid(ga_ref[...].astype(F32)) * ya + _sigmoid(gb_ref[...].astype(F32)) * yb
    o_ref[...] = y.astype(o_ref.dtype)


def _gated(o_a, o_b, w_a, w_b, h_g, l, tm, tn):
    m = o_a.shape[0]
    d = w_a.shape[2]
    tm = min(tm, m)
    ngb = d // tn
    return pl.pallas_call(
        _gated_kernel,
        out_shape=SDS((m, d), BF16),
        grid=(m // tm, ngb),
        in_specs=[pl.BlockSpec((tm, DA_V_W), lambda i, j: (i, 0)),
                  pl.BlockSpec((tm, MLA_O_W), lambda i, j: (i, 0)),
                  pl.BlockSpec((None, DA_V_W, tn), lambda i, j: (l, 0, j)),
                  pl.BlockSpec((None, MLA_O_W, tn), lambda i, j: (l, 0, j)),
                  pl.BlockSpec((tm, tn), lambda i, j: (i, j)),
                  pl.BlockSpec((tm, tn), lambda i, j: (i, ngb + j))],
        out_specs=pl.BlockSpec((tm, tn), lambda i, j: (i, j)),
        compiler_params=_cparams(2),
        name="gated",
    )(o_a, o_b, w_a, w_b, h_g, h_g)


def _layer_norm(z, g, b):
    mu = jnp.mean(z, axis=-1, keepdims=True)
    zc = z - mu
    var = jnp.mean(zc * zc, axis=-1, keepdims=True)
    return zc * lax.rsqrt(var + NORM_EPS) * g + b


def _split_bf16(x):
    hi = x.astype(BF16)
    lo = (x - hi.astype(F32)).astype(BF16)
    return hi, lo


def _route_record(lg):
    lane = lax.broadcasted_iota(I32, lg.shape, 1)
    lane_f = lane.astype(F32)
    big = float(ROUTE_W)
    gmask = lane < N_GROUPS
    g_max = jnp.max(jnp.where(gmask, lg, NEG_INF), axis=1, keepdims=True)
    g_idx = jnp.min(jnp.where(gmask & (lg == g_max), lane_f, big), axis=1, keepdims=True)
    g_w = 1.0 / jnp.sum(jnp.where(gmask, jnp.exp(lg - g_max), 0.0), axis=1, keepdims=True)

    lo = N_GROUPS + EXPERTS_PER_GROUP * g_idx
    sel = (lane_f >= lo) & (lane_f < lo + EXPERTS_PER_GROUP)
    e_max = jnp.max(jnp.where(sel, lg, NEG_INF), axis=1, keepdims=True)
    pe = jnp.where(sel, jnp.exp(lg - e_max), 0.0)
    prob = pe / jnp.sum(pe, axis=1, keepdims=True)
    p1 = jnp.max(jnp.where(sel, prob, -1.0), axis=1, keepdims=True)
    i1 = jnp.min(jnp.where(sel & (prob == p1), lane_f, big), axis=1, keepdims=True)
    sel2 = sel & (lane_f != i1)
    p2 = jnp.max(jnp.where(sel2, prob, -1.0), axis=1, keepdims=True)
    i2 = jnp.min(jnp.where(sel2 & (prob == p2), lane_f, big), axis=1, keepdims=True)
    tot = p1 + p2
    w1 = g_w * (p1 / tot)
    w2 = g_w * (p2 / tot)
    rec = jnp.where(lane == 0, i1 - N_GROUPS,
          jnp.where(lane == 1, i2 - N_GROUPS,
          jnp.where(lane == 2, w1, jnp.where(lane == 3, w2, 0.0))))
    return rec


def _outproj_kernel(y_ref, w_ref, x_ref, g_ref, b_ref, wr_ref, br_ref,
                    xo_ref, xt_ref, route_ref, *, alpha):
    mix = jnp.dot(y_ref[...], w_ref[...].astype(BF16), preferred_element_type=F32)
    xn = _layer_norm(alpha * x_ref[...] + mix, g_ref[...], b_ref[...])
    xo_ref[...] = xn
    _wide_to_tiled(xn, xt_ref)
    xh, xl = _split_bf16(xn)
    wh, wl = _split_bf16(wr_ref[...])
    lg = (jnp.dot(xh, wh, preferred_element_type=F32) + jnp.dot(xl, wh, preferred_element_type=F32)
          + jnp.dot(xh, wl, preferred_element_type=F32) + br_ref[...])
    route_ref[...] = _route_record(lg)


def _outproj_ln_route(y, w_out, x, ln_g, ln_b, w_r, b_r, l, alpha, tm):
    m, d = x.shape
    tm = min(tm, m)
    const = dict(pipeline_mode=pl.Buffered(1))
    return pl.pallas_call(
        functools.partial(_outproj_kernel, alpha=alpha),
        out_shape=(SDS((m, d), F32), SDS((m * ROW_TILES, LANES), U32), SDS((m, ROUTE_W), F32)),
        grid=(m // tm,),
        in_specs=[pl.BlockSpec((tm, d), lambda i: (i, 0)),
                  pl.BlockSpec((None, d, d), lambda i: (l, 0, 0), **const),
                  pl.BlockSpec((tm, d), lambda i: (i, 0)),
                  pl.BlockSpec((None, 1, d), lambda i: (l, 0, 0)),
                  pl.BlockSpec((None, 1, d), lambda i: (l, 0, 0)),
                  pl.BlockSpec((None, d, ROUTE_W), lambda i: (0, 0, 0)),
                  pl.BlockSpec((None, 1, ROUTE_W), lambda i: (0, 0, 0))],
        out_specs=(pl.BlockSpec((tm, d), lambda i: (i, 0)),
                   pl.BlockSpec((tm * ROW_TILES, LANES), lambda i: (i, 0)),
                   pl.BlockSpec((tm, ROUTE_W), lambda i: (i, 0))),
        compiler_params=_cparams(1),
        name="outproj_ln_route",
    )(y, w_out, x, ln_g, ln_b, w_r, b_r)


def _row_copy(src_hbm, row, dst, r, sem):
    src = src_hbm.at[pl.ds(pl.multiple_of(row * ROW_TILES, ROW_TILES), ROW_TILES), :]
    return pltpu.make_async_copy(src, dst.at[pl.ds(pl.multiple_of(r * ROW_TILES, ROW_TILES), ROW_TILES), :], sem)


HI_MASK = 0xFFFF0000


def _tiled_to_wide(ref, n):
    words = [ref[pl.ds(s, n, stride=ROW_TILES), :] for s in range(ROW_TILES)]
    lo = [pltpu.bitcast(w << 16, F32) for w in words]
    hi = [pltpu.bitcast(w & jnp.uint32(HI_MASK), F32) for w in words]
    return jnp.concatenate(lo + hi, axis=1)


def _wide_to_tiled(val, ref):
    n = val.shape[0]
    bits = pltpu.bitcast(val.astype(BF16).astype(F32), U32)
    for s in range(ROW_TILES):
        lo = bits[:, s * LANES:(s + 1) * LANES] >> 16
        hi = bits[:, (ROW_TILES + s) * LANES:(ROW_TILES + s + 1) * LANES]
        ref[pl.ds(s, n, stride=ROW_TILES), :] = hi | lo


def _gather_rows(src_hbm, idx_ref, base, dst, sem, n):
    assert n % GATHER_UNROLL == 0

    def issue(i, c):
        for u in range(GATHER_UNROLL):
            r = i * GATHER_UNROLL + u
            _row_copy(src_hbm, idx_ref[base + r], dst, r, sem).start(priority=u % DMA_QUEUES)
        return c

    lax.fori_loop(0, n // GATHER_UNROLL, issue, 0)


def _gather_wait(src_hbm, dst, sem, n):
    def wait(r, c):
        _row_copy(src_hbm, 0, dst, r, sem).wait()
        return c

    lax.fori_loop(0, n, wait, 0, unroll=GATHER_UNROLL)


def _moe_kernel(bs_ref, nb_ref, nu_ref, tok_ref, x_hbm, wg_ref, wu_ref, wd_ref, y_hbm,
                xbuf, obuf, gsem, osem, *, bm, n_blocks):
    e = pl.program_id(0)
    n_used = nu_ref[0]

    def out_copy(g, slot):
        rows = pl.ds(pl.multiple_of(g * (bm * ROW_TILES), bm * ROW_TILES), bm * ROW_TILES)
        return pltpu.make_async_copy(obuf.at[slot], y_hbm.at[rows, :], osem.at[slot])

    @pl.when(e == 0)
    def _():
        for k in range(GATHER_AHEAD):
            _gather_rows(x_hbm, tok_ref, min(k, n_blocks - 1) * bm, xbuf.at[k], gsem.at[k], bm)

    def body(j, c):
        g = bs_ref[e] + j
        slot = lax.rem(g, 2)
        gslot = lax.rem(g, GATHER_SLOTS)
        nslot = lax.rem(g + GATHER_AHEAD, GATHER_SLOTS)

        _gather_wait(x_hbm, xbuf.at[gslot], gsem.at[gslot], bm)

        @pl.when(g >= 2)
        def _():
            out_copy(g - 2, slot).wait()

        g_next = jnp.minimum(g + GATHER_AHEAD, n_blocks - 1)
        for r in range(bm):
            _row_copy(x_hbm, tok_ref[g_next * bm + r], xbuf.at[nslot], r,
                      gsem.at[nslot]).start(priority=r % DMA_QUEUES)

        xb = _tiled_to_wide(xbuf.at[gslot], bm).astype(BF16)
        gt = _dot_nt(xb, wg_ref[...].astype(BF16))
        up = _dot_nt(xb, wu_ref[...].astype(BF16))
        hid = ((gt * _sigmoid(gt)) * up).astype(BF16)
        _wide_to_tiled(jnp.dot(hid, wd_ref[...].astype(BF16), preferred_element_type=F32),
                       obuf.at[slot])
        out_copy(g, slot).start()
        return c

    lax.fori_loop(0, nb_ref[e], body, 0)

    @pl.when(e == pl.num_programs(0) - 1)
    def _():
        @pl.when(n_used >= 2)
        def _():
            out_copy(n_used - 2, lax.rem(n_used - 2, 2)).wait()

        out_copy(n_used - 1, lax.rem(n_used - 1, 2)).wait()
        for k in range(GATHER_AHEAD):
            dslot = lax.rem(n_used + k, GATHER_SLOTS)
            _gather_wait(x_hbm, xbuf.at[dslot], gsem.at[dslot], bm)

        obuf[0] = jnp.zeros(obuf.shape[1:], obuf.dtype)

        def fill(g, c):
            cp = out_copy(g, 0)
            cp.start()
            cp.wait()
            return c

        lax.fori_loop(n_used, n_blocks, fill, 0)


def _moe_ffn(x_tiled, bstart, nblk, n_used, row_tok, wgt, wut, wd, l, bm):
    f, d = wd.shape[2], wd.shape[3]
    assert d == 2 * ROW_TILES * LANES
    n_blocks = row_tok.shape[0] // bm
    w_spec = pl.BlockSpec((None, None, f, d), lambda e, bs, nb, nu, tok: (l, e, 0, 0))
    grid_spec = pltpu.PrefetchScalarGridSpec(
        num_scalar_prefetch=4,
        grid=(N_EXPERTS,),
        in_specs=[pl.BlockSpec(memory_space=pl.ANY), w_spec, w_spec, w_spec],
        out_specs=pl.BlockSpec(memory_space=pl.ANY),
        scratch_shapes=[pltpu.VMEM((GATHER_SLOTS, bm * ROW_TILES, LANES), U32),
                        pltpu.VMEM((2, bm * ROW_TILES, LANES), U32),
                        pltpu.SemaphoreType.DMA((GATHER_SLOTS,)), pltpu.SemaphoreType.DMA((2,))],
    )
    return pl.pallas_call(
        functools.partial(_moe_kernel, bm=bm, n_blocks=n_blocks),
        out_shape=SDS((n_blocks * bm * ROW_TILES, LANES), U32),
        grid_spec=grid_spec,
        compiler_params=_cparams(1),
        name="moe_ffn",
    )(bstart, nblk, n_used, row_tok, x_tiled, wgt, wut, wd)


def _combine_kernel(pos_ref, yb_hbm, x_ref, route_ref, g_ref, b_ref, xo_ref, xbo_ref,
                    buf, sems, *, tm, t_total, alpha):
    i = pl.program_id(0)
    slot = lax.rem(i, 2)

    def gather(step, s):
        for k in range(TOP_K):
            _gather_rows(yb_hbm, pos_ref, k * t_total + step * tm, buf.at[s, k],
                         sems.at[s * TOP_K + k], tm)

    @pl.when(i == 0)
    def _():
        gather(0, 0)

    @pl.when(i + 1 < pl.num_programs(0))
    def _():
        gather(i + 1, 1 - slot)

    for k in range(TOP_K):
        _gather_wait(yb_hbm, buf.at[slot, k], sems.at[slot * TOP_K + k], tm)

    rec = route_ref[...]
    z = (alpha * x_ref[...] + rec[:, 2:3] * _tiled_to_wide(buf.at[slot, 0], tm)
         + rec[:, 3:4] * _tiled_to_wide(buf.at[slot, 1], tm))
    xn = _layer_norm(z, g_ref[...], b_ref[...])
    xo_ref[...] = xn
    xbo_ref[...] = xn.astype(BF16)


def _combine_ln(pos_k, yb, x, route, ln_g, ln_b, l, alpha, tm):
    t, d = x.shape
    tm = min(tm, t)
    grid_spec = pltpu.PrefetchScalarGridSpec(
        num_scalar_prefetch=1,
        grid=(t // tm,),
        in_specs=[pl.BlockSpec(memory_space=pl.ANY),
                  pl.BlockSpec((tm, d), lambda i, pos: (i, 0)),
                  pl.BlockSpec((tm, ROUTE_W), lambda i, pos: (i, 0)),
                  pl.BlockSpec((None, 1, d), lambda i, pos: (l, 0, 0)),
                  pl.BlockSpec((None, 1, d), lambda i, pos: (l, 0, 0))],
        out_specs=(pl.BlockSpec((tm, d), lambda i, pos: (i, 0)),
                   pl.BlockSpec((tm, d), lambda i, pos: (i, 0))),
        scratch_shapes=[pltpu.VMEM((2, TOP_K, tm * ROW_TILES, LANES), U32),
                        pltpu.SemaphoreType.DMA((2 * TOP_K,))],
    )
    return pl.pallas_call(
        functools.partial(_combine_kernel, tm=tm, t_total=t, alpha=alpha),
        out_shape=(SDS((t, d), F32), SDS((t, d), BF16)),
        grid_spec=grid_spec,
        compiler_params=_cparams(1),
        name="combine_ln",
    )(pos_k, yb, x, route, ln_g, ln_b)


def _route_meta(e_ids, bm):
    t = e_ids.shape[0]
    a = t * TOP_K
    flat_e = e_ids.reshape(a)
    onehot = (flat_e[:, None] == jnp.arange(N_EXPERTS, dtype=I32)[None, :]).astype(I32)
    csum = jnp.cumsum(onehot, axis=0)
    rank = jnp.take_along_axis(csum, flat_e[:, None], axis=1)[:, 0] - 1
    counts = csum[-1]
    nblk = (counts + bm - 1) // bm
    bend = jnp.cumsum(nblk)
    bstart = bend - nblk
    dest = (bstart[flat_e] * bm + rank).astype(I32)
    n_blocks = a // bm + N_EXPERTS
    row_tok = jnp.zeros((n_blocks * bm,), I32).at[dest].set(jnp.arange(a, dtype=I32) // TOP_K)
    pos_k = dest.reshape(t, TOP_K).T.reshape(a)
    return pos_k, row_tok, bstart.astype(I32), nblk.astype(I32), bend[-1:].astype(I32)


def _swap_halves(w, axis):
    half = w.shape[axis] // 2
    lo = lax.slice_in_dim(w, 0, half, axis=axis)
    hi = lax.slice_in_dim(w, half, 2 * half, axis=axis)
    return jnp.concatenate([-hi, lo], axis=axis)


def _prep_layer_weights(wk_t, w_uq_l, w_ukv_l, w_rg_l, b_rg_l, w_re_l, b_re_l):
    d = wk_t.shape[1]
    w_krt = jnp.concatenate([wk_t, _swap_halves(wk_t, 0)], axis=0)
    wq = w_uq_l.reshape(MLA_Q_RANK, MLA_HEADS, MLA_NOPE_DIM + MLA_ROPE_DIM)
    wq_rope = wq[..., MLA_NOPE_DIM:]
    w_uq2 = jnp.concatenate([wq[..., :MLA_NOPE_DIM], wq_rope, _swap_halves(wq_rope, 2)],
                            axis=-1).reshape(MLA_Q_RANK, MLA_HEADS * MLA_QCAT)
    wkv = w_ukv_l.reshape(MLA_KV_RANK, MLA_HEADS, MLA_NOPE_DIM + MLA_V_DIM)
    w_ukv2 = jnp.concatenate([wkv[..., :MLA_NOPE_DIM].reshape(MLA_KV_RANK, -1),
                              wkv[..., MLA_NOPE_DIM:].reshape(MLA_KV_RANK, -1)], axis=1)
    pad = ROUTE_W - N_GROUPS - N_EXPERTS
    w_r = jnp.concatenate([w_rg_l, w_re_l, jnp.zeros((d, pad), F32)], axis=1)[None]
    b_r = jnp.concatenate([b_rg_l, b_re_l, jnp.zeros((pad,), F32)])[None, None]
    return w_krt, w_uq2, w_ukv2, w_r, b_r


def _rope_tables(seq):
    inv = ROPE_THETA ** (-jnp.arange(0, MLA_ROPE_DIM, 2, dtype=F32) / MLA_ROPE_DIM)
    ang = jnp.arange(seq, dtype=F32)[:, None] * inv[None, :]
    return jnp.cos(ang), jnp.sin(ang)


def kernel(x, w_in, da_lambda, da_subln_g, mla_q_norm_g, mla_w_uq, mla_kv_norm_g, mla_w_ukv, w_branch_a, w_branch_b, w_out, rel_bias, ln1_g, ln1_b, router_w_group, router_b_group, router_w_expert, router_b_expert, expert_w_gate, expert_w_up, expert_w_down, ln2_g, ln2_b):
    batch, seq, d = x.shape
    depth = w_in.shape[0]
    t = batch * seq
    alpha = (2 * depth) ** 0.25
    main_w = DA_QK_W * 2 + DA_V_W + MLA_Q_RANK + MLA_KV_RANK
    gate0 = main_w + MLA_ROPE_DIM

    cos, sin = _rope_tables(seq)
    tabk = jnp.concatenate([cos, cos, sin, sin], axis=1)
    q_scale = (MLA_NOPE_DIM + MLA_ROPE_DIM) ** -0.5
    tabq = q_scale * jnp.concatenate([jnp.ones((seq, MLA_NOPE_DIM), F32), tabk], axis=1)

    subln_g = da_subln_g[:, None, :]
    ln1_g3, ln1_b3 = ln1_g[:, None, :], ln1_b[:, None, :]
    ln2_g3, ln2_b3 = ln2_g[:, None, :], ln2_b[:, None, :]

    wt_in = jnp.swapaxes(w_in, 1, 2)
    wgt = jnp.swapaxes(expert_w_gate, 2, 3)
    wut = jnp.swapaxes(expert_w_up, 2, 3)

    xf = x.reshape(t, d)
    xb = xf.astype(BF16)
    for l in range(depth):
        lam_init = 0.8 - 0.6 * math.exp(-0.3 * l)
        w_krt, w_uq2, w_ukv2, w_r, b_r = _prep_layer_weights(
            wt_in[l, main_w:gate0, :], mla_w_uq[l], mla_w_ukv[l], router_w_group[l],
            router_b_group[l], router_w_expert[l], router_b_expert[l])

        h_main = _matmul_nt(xb, wt_in, l, 0, main_w, 4096, 256, BF16, "mm_main")
        h_g = _matmul_nt(xb, wt_in, l, gate0, 2 * d, 2048, 512, BF16, "mm_gates")
        kr2 = _krope(xb, w_krt, tabk, seq, 512)
        o_a = _da_attention(h_main, rel_bias, da_lambda, subln_g, l, lam_init, batch, seq)
        q_cat = _uq(h_main, mla_q_norm_g[l][None], w_uq2, tabq, seq, 512)
        kv = _ukv(h_main, mla_kv_norm_g[l][None], w_ukv2, 512)
        o_b = _mla_attention(q_cat, kv, kr2, batch, seq)
        y = _gated(o_a, o_b, w_branch_a, w_branch_b, h_g, l, 1024, 512)
        x1, x1_tiled, route = _outproj_ln_route(y, w_out, xf, ln1_g3, ln1_b3, w_r, b_r, l, alpha, 256)

        e_ids = route[:, :TOP_K].astype(I32)
        pos_k, row_tok, bstart, nblk, n_used = _route_meta(e_ids, MOE_BLOCK)
        yb = _moe_ffn(x1_tiled, bstart, nblk, n_used, row_tok, wgt, wut, expert_w_down, l, MOE_BLOCK)
        xf, xb = _combine_ln(pos_k, yb, x1, route, ln2_g3, ln2_b3, l, alpha, 256)
    return xf.reshape(batch, seq, d)
```

```python
import functools
import math

import jax
import jax.numpy as jnp
from jax import lax
from jax.experimental import pallas as pl
from jax.experimental.pallas import tpu as pltpu

F32 = jnp.float32
BF16 = jnp.bfloat16
I32 = jnp.int32
U32 = jnp.uint32
SDS = jax.ShapeDtypeStruct

DA_HEADS = 8
DA_HEAD_DIM = 64
DA_V_DIM = 2 * DA_HEAD_DIM
MLA_HEADS = 8
MLA_Q_RANK = 768
MLA_KV_RANK = 512
MLA_NOPE_DIM = 128
MLA_ROPE_DIM = 64
MLA_V_DIM = 128
ROPE_THETA = 10000.0
REL_BUCKETS = 32
REL_MAX_DIST = 128
REL_MAX_EXACT = REL_BUCKETS // 2
N_GROUPS = 4
EXPERTS_PER_GROUP = 8
N_EXPERTS = N_GROUPS * EXPERTS_PER_GROUP
TOP_K = 2
NORM_EPS = 1e-5
NEG_INF = -1e30

DA_QK_W = DA_HEADS * 2 * DA_HEAD_DIM
DA_V_W = DA_HEADS * DA_V_DIM
MLA_O_W = MLA_HEADS * MLA_V_DIM
MLA_QCAT = 2 * MLA_NOPE_DIM

LANES = 128
SUBLANES = 8
VMEM_LIMIT = 56 * 1024 * 1024

ATT_K_BLOCK = 512
DA_Q_BLOCK = 256
MOE_BLOCK = 256
ROUTE_W = LANES
GATHER_UNROLL = 8
ROW_TILES = 8
HEADS_PER_STEP = 8
DMA_QUEUES = 2
GATHER_AHEAD = 5
GATHER_SLOTS = GATHER_AHEAD + 1


def _cparams(n_axes):
    return pltpu.CompilerParams(dimension_semantics=("arbitrary",) * n_axes,
                                vmem_limit_bytes=VMEM_LIMIT)


def _dot_nt(a, b):
    return lax.dot_general(a, b, (((1,), (1,)), ((), ())), preferred_element_type=F32)


def _mm_nt_kernel(a_ref, w_ref, o_ref):
    o_ref[...] = _dot_nt(a_ref[...], w_ref[0].astype(BF16)).astype(o_ref.dtype)


def _matmul_nt(a, wt3, l, row0, n_rows, tm, tn, out_dtype, name):
    m, k = a.shape
    tm = min(tm, m)
    assert m % tm == 0 and n_rows % tn == 0
    if row0 % tn == 0:
        r0 = row0 // tn
        w_spec = pl.BlockSpec((1, tn, k), lambda i, j: (l, r0 + j, 0))
    else:
        w_spec = pl.BlockSpec((pl.Element(1), pl.Element(tn), pl.Element(k)),
                              lambda i, j: (l, pl.multiple_of(row0 + j * tn, SUBLANES), 0))
    return pl.pallas_call(
        _mm_nt_kernel,
        out_shape=SDS((m, n_rows), out_dtype),
        grid=(m // tm, n_rows // tn),
        in_specs=[pl.BlockSpec((tm, k), lambda i, j: (i, 0)), w_spec],
        out_specs=pl.BlockSpec((tm, tn), lambda i, j: (i, j)),
        compiler_params=_cparams(2),
        name=name,
    )(a, wt3)


def _krope_kernel(a_ref, w_ref, tab_ref, o_ref):
    t = _dot_nt(a_ref[...], w_ref[...].astype(BF16))
    t = t * tab_ref[...]
    o_ref[...] = (t + pltpu.roll(t, MLA_ROPE_DIM, axis=1)).astype(o_ref.dtype)


def _krope(xb, w_kr, tabk, seq, tm):
    m, k = xb.shape
    tm = min(tm, seq)
    nsb = seq // tm
    return pl.pallas_call(
        _krope_kernel,
        out_shape=SDS((m, LANES), BF16),
        grid=(m // tm,),
        in_specs=[pl.BlockSpec((tm, k), lambda i: (i, 0)),
                  pl.BlockSpec((LANES, k), lambda i: (0, 0)),
                  pl.BlockSpec((tm, LANES), lambda i: (i % nsb, 0))],
        out_specs=pl.BlockSpec((tm, LANES), lambda i: (i, 0)),
        compiler_params=_cparams(1),
        name="krope",
    )(xb, w_kr, tabk)


def _rms(c, g):
    return c * lax.rsqrt(jnp.mean(c * c, axis=-1, keepdims=True) + NORM_EPS) * g


def _uq_kernel(c_ref, g_ref, w_ref, tab_ref, o_ref):
    n = _rms(c_ref[...].astype(F32), g_ref[...])
    acc = jnp.dot(n.astype(BF16), w_ref[...].astype(BF16), preferred_element_type=F32)
    tab = tab_ref[...]
    for h in range(MLA_HEADS):
        sl = slice(h * MLA_QCAT, (h + 1) * MLA_QCAT)
        o_ref[:, sl] = (acc[:, sl] * tab).astype(o_ref.dtype)


def _uq(h_main, g, w_uq2, tabq, seq, tm):
    m = h_main.shape[0]
    tm = min(tm, seq)
    nsb = seq // tm
    cq_blk = (DA_QK_W * 2 + DA_V_W) // MLA_Q_RANK
    n_out = MLA_HEADS * MLA_QCAT
    return pl.pallas_call(
        _uq_kernel,
        out_shape=SDS((m, n_out), BF16),
        grid=(m // tm,),
        in_specs=[pl.BlockSpec((tm, MLA_Q_RANK), lambda i: (i, cq_blk)),
                  pl.BlockSpec((1, MLA_Q_RANK), lambda i: (0, 0)),
                  pl.BlockSpec((MLA_Q_RANK, n_out), lambda i: (0, 0)),
                  pl.BlockSpec((tm, MLA_QCAT), lambda i: (i % nsb, 0))],
        out_specs=pl.BlockSpec((tm, n_out), lambda i: (i, 0)),
        compiler_params=_cparams(1),
        name="mla_uq",
    )(h_main, g, w_uq2, tabq)


def _ukv_kernel(c0_ref, c1_ref, g_ref, w_ref, o_ref):
    c = jnp.concatenate([c0_ref[...], c1_ref[...]], axis=1).astype(F32)
    n = _rms(c, g_ref[...])
    o_ref[...] = jnp.dot(n.astype(BF16), w_ref[...].astype(BF16),
                         preferred_element_type=F32).astype(o_ref.dtype)


def _ukv(h_main, g, w_ukv2, tm):
    m = h_main.shape[0]
    tm = min(tm, m)
    half = MLA_KV_RANK // 2
    b0 = (DA_QK_W * 2 + DA_V_W + MLA_Q_RANK) // half
    n_out = w_ukv2.shape[1]
    return pl.pallas_call(
        _ukv_kernel,
        out_shape=SDS((m, n_out), BF16),
        grid=(m // tm,),
        in_specs=[pl.BlockSpec((tm, half), lambda i: (i, b0)),
                  pl.BlockSpec((tm, half), lambda i: (i, b0 + 1)),
                  pl.BlockSpec((1, MLA_KV_RANK), lambda i: (0, 0)),
                  pl.BlockSpec((MLA_KV_RANK, n_out), lambda i: (0, 0))],
        out_specs=pl.BlockSpec((tm, n_out), lambda i: (i, 0)),
        compiler_params=_cparams(1),
        name="mla_ukv",
    )(h_main, h_main, g, w_ukv2)


def _flash_init(m_scr, acc_scr):
    m_scr[...] = jnp.full(m_scr.shape, NEG_INF, F32)
    acc_scr[...] = jnp.zeros(acc_scr.shape, F32)


def _flash_update(s, v, m_scr, acc_scr):
    v1 = jnp.concatenate([v, jnp.ones(v.shape, v.dtype)], axis=1)
    m_prev = m_scr[...]
    m_new = jnp.maximum(m_prev, jnp.max(s, axis=1, keepdims=True))
    p = jnp.exp(s - jnp.concatenate([m_new] * (s.shape[1] // LANES), axis=1))
    alpha = jnp.exp(m_prev - m_new)
    acc_scr[...] = (jnp.concatenate([alpha, alpha], axis=1) * acc_scr[...]
                    + jnp.dot(p.astype(BF16), v1, preferred_element_type=F32))
    m_scr[...] = m_new


def _flash_result(acc_scr):
    acc = acc_scr[...]
    return acc[:, :LANES] / acc[:, LANES:]


def _t5_bias_blocks(rb_ref, h, bias_scr, tq, tk):
    row = lax.broadcasted_iota(I32, (tq, tk), 0)
    col = lax.broadcasted_iota(I32, (tq, tk), 1)
    far = rb_ref[REL_BUCKETS - 1, h]
    for d in range(bias_scr.shape[0]):
        n = row - col + d * tq
        nn = jnp.maximum(n, 0)
        nf = jnp.maximum(nn, 1).astype(F32)
        large = REL_MAX_EXACT + (jnp.log(nf / REL_MAX_EXACT) / math.log(REL_MAX_DIST / REL_MAX_EXACT)
                                 * (REL_BUCKETS - REL_MAX_EXACT)).astype(I32)
        large = jnp.minimum(large, REL_BUCKETS - 1)
        bucket = jnp.where(nn < REL_MAX_EXACT, nn, large)
        val = jnp.zeros((tq, tk), F32)
        for bb in range(REL_BUCKETS):
            val = jnp.where(bucket == bb, rb_ref[bb, h], val)
        bias_scr[d] = jnp.where(n >= 0, val - far, NEG_INF)


def _da_kernel(rb_ref, lamv_ref, q_ref, k_ref, v_ref, g_ref, o_ref,
               bias_scr, m_scr, acc_scr, *, tq, tk, lam_init):
    hp = pl.program_id(0)
    b = pl.program_id(1)
    qi = pl.program_id(2)
    ratio = tk // tq
    jd = qi // ratio
    par = qi % ratio
    hd = 2 * DA_HEAD_DIM
    heads = range(HEADS_PER_STEP)

    @pl.when((b == 0) & (qi == 0))
    def _():
        for hh in heads:
            _t5_bias_blocks(rb_ref, hp * HEADS_PER_STEP + hh, bias_scr.at[hh], tq, tk)

    scale = DA_HEAD_DIM ** -0.5
    q_all = q_ref[...]
    q2 = []
    for hh in heads:
        q = q_all[:, hh * hd:(hh + 1) * hd]
        lane = lax.broadcasted_iota(I32, q.shape, 1)
        zero = jnp.zeros_like(q)
        q2.append(jnp.concatenate([jnp.where(lane < DA_HEAD_DIM, q, zero),
                                   jnp.where(lane >= DA_HEAD_DIM, q, zero)], axis=0) * scale)
        _flash_init(m_scr.at[hh], acc_scr.at[hh])

    def block(j, bias_idx):
        off = pl.multiple_of(j * tk, tk)
        kb = k_ref[pl.ds(off, tk), :]
        vb = v_ref[pl.ds(off, tk), :]
        for hh in heads:
            s = _dot_nt(q2[hh], kb[:, hh * hd:(hh + 1) * hd])
            if bias_idx is not None:
                bias = bias_scr[hh, bias_idx]
                s = s + jnp.concatenate([bias, bias], axis=0)
            _flash_update(s, vb[:, hh * DA_V_DIM:(hh + 1) * DA_V_DIM], m_scr.at[hh], acc_scr.at[hh])

    def far_body(j, c):
        block(j, None)
        return c

    lax.fori_loop(0, jd - 1, far_body, 0)

    @pl.when((jd >= 1) & (par == 0))
    def _():
        block(jd - 1, ratio)

    @pl.when((jd >= 1) & (par != 0))
    def _():
        block(jd - 1, None)

    block(jd, par)

    lamv = lamv_ref[...]
    lam = (jnp.exp(jnp.sum(lamv[0:1] * lamv[1:2], axis=1, keepdims=True))
           - jnp.exp(jnp.sum(lamv[2:3] * lamv[3:4], axis=1, keepdims=True)) + lam_init)
    for hh in heads:
        o12 = _flash_result(acc_scr.at[hh])
        o = o12[:tq] - lam * o12[tq:]
        o = o * lax.rsqrt(jnp.mean(o * o, axis=1, keepdims=True) + NORM_EPS) * g_ref[...] * (1.0 - lam_init)
        o_ref[:, hh * DA_V_DIM:(hh + 1) * DA_V_DIM] = o.astype(o_ref.dtype)


def _da_attention(h_main, rel_bias, da_lambda, subln_g, l, lam_init, batch, seq):
    tq, tk = DA_Q_BLOCK, ATT_K_BLOCK
    ratio = tk // tq
    assert seq % tk == 0 and tk % tq == 0 and tq + 1 >= REL_MAX_DIST
    nq = seq // tq
    hw = HEADS_PER_STEP * 2 * DA_HEAD_DIM
    vw = HEADS_PER_STEP * DA_V_DIM
    kcol = DA_QK_W // hw
    vcol = 2 * DA_QK_W // vw
    return pl.pallas_call(
        functools.partial(_da_kernel, tq=tq, tk=tk, lam_init=lam_init),
        out_shape=SDS((batch * seq, DA_V_W), BF16),
        grid=(DA_HEADS // HEADS_PER_STEP, batch, nq),
        in_specs=[pl.BlockSpec(memory_space=pltpu.SMEM),
                  pl.BlockSpec((None, 4, DA_HEAD_DIM), lambda h, b, qi: (l, 0, 0)),
                  pl.BlockSpec((tq, hw), lambda h, b, qi: (b * nq + qi, h)),
                  pl.BlockSpec((seq, hw), lambda h, b, qi: (b, kcol + h)),
                  pl.BlockSpec((seq, vw), lambda h, b, qi: (b, vcol + h)),
                  pl.BlockSpec((None, 1, DA_V_DIM), lambda h, b, qi: (l, 0, 0))],
        out_specs=pl.BlockSpec((tq, vw), lambda h, b, qi: (b * nq + qi, h)),
        scratch_shapes=[pltpu.VMEM((HEADS_PER_STEP, ratio + 1, tq, tk), F32),
                        pltpu.VMEM((HEADS_PER_STEP, 2 * tq, LANES), F32),
                        pltpu.VMEM((HEADS_PER_STEP, 2 * tq, 2 * LANES), F32)],
        compiler_params=_cparams(3),
        name="da_attn",
    )(rel_bias, da_lambda, h_main, h_main, h_main, subln_g)


def _mla_kernel(q_ref, kn_ref, kr_ref, v_ref, o_ref, m_scr, acc_scr, *, blk):
    qi = pl.program_id(2)
    heads = range(HEADS_PER_STEP)
    q_all = q_ref[...]
    q = [q_all[:, hh * MLA_QCAT:(hh + 1) * MLA_QCAT] for hh in heads]
    for hh in heads:
        _flash_init(m_scr.at[hh], acc_scr.at[hh])

    def block(j, masked):
        off = pl.multiple_of(j * blk, blk)
        kn = kn_ref[pl.ds(off, blk), :]
        kr = kr_ref[pl.ds(off, blk), :]
        vb = v_ref[pl.ds(off, blk), :]
        for hh in heads:
            kcat = jnp.concatenate([kn[:, hh * MLA_NOPE_DIM:(hh + 1) * MLA_NOPE_DIM], kr], axis=1)
            s = _dot_nt(q[hh], kcat)
            if masked:
                row = lax.broadcasted_iota(I32, s.shape, 0)
                col = lax.broadcasted_iota(I32, s.shape, 1)
                s = jnp.where(col <= row, s, NEG_INF)
            _flash_update(s, vb[:, hh * MLA_V_DIM:(hh + 1) * MLA_V_DIM], m_scr.at[hh], acc_scr.at[hh])

    def far_body(j, c):
        block(j, False)
        return c

    lax.fori_loop(0, qi, far_body, 0)
    block(qi, True)
    for hh in heads:
        o_ref[:, hh * MLA_V_DIM:(hh + 1) * MLA_V_DIM] = _flash_result(acc_scr.at[hh]).astype(o_ref.dtype)


def _mla_attention(q_cat, kv, kr2, batch, seq):
    blk = min(ATT_K_BLOCK, seq)
    nq = seq // blk
    hps = HEADS_PER_STEP
    vcol = MLA_HEADS // hps
    return pl.pallas_call(
        functools.partial(_mla_kernel, blk=blk),
        out_shape=SDS((batch * seq, MLA_O_W), BF16),
        grid=(batch, MLA_HEADS // hps, nq),
        in_specs=[pl.BlockSpec((blk, hps * MLA_QCAT), lambda b, h, qi: (b * nq + qi, h)),
                  pl.BlockSpec((seq, hps * MLA_NOPE_DIM), lambda b, h, qi: (b, h)),
                  pl.BlockSpec((seq, LANES), lambda b, h, qi: (b, 0)),
                  pl.BlockSpec((seq, hps * MLA_V_DIM), lambda b, h, qi: (b, vcol + h))],
        out_specs=pl.BlockSpec((blk, hps * MLA_V_DIM), lambda b, h, qi: (b * nq + qi, h)),
        scratch_shapes=[pltpu.VMEM((hps, blk, LANES), F32),
                        pltpu.VMEM((hps, blk, 2 * LANES), F32)],
        compiler_params=_cparams(3),
        name="mla_attn",
    )(q_cat, kv, kr2, kv)


def _sigmoid(x):
    return 1.0 / (1.0 + jnp.exp(-x))


def _gated_kernel(oa_ref, ob_ref, wa_ref, wb_ref, ga_ref, gb_ref, o_ref):
    ya = jnp.dot(oa_ref[...], wa_ref[...].astype(BF16), preferred_element_type=F32)
    yb = jnp.dot(ob_ref[...], wb_ref[...].astype(BF16), preferred_element_type=F32)
    y = _sigmoid(ga_ref[...].astype(F32)) * ya + _sigmoid(gb_ref[...].astype(F32)) * yb
    o_ref[...] = y.astype(o_ref.dtype)


def _gated(o_a, o_b, w_a, w_b, h_g, l, tm, tn):
    m = o_a.shape[0]
    d = w_a.shape[2]
    tm = min(tm, m)
    ngb = d // tn
    return pl.pallas_call(
        _gated_kernel,
        out_shape=SDS((m, d), BF16),
        grid=(m // tm, ngb),
        in_specs=[pl.BlockSpec((tm, DA_V_W), lambda i, j: (i, 0)),
                  pl.BlockSpec((tm, MLA_O_W), lambda i, j: (i, 0)),
                  pl.BlockSpec((None, DA_V_W, tn), lambda i, j: (l, 0, j)),
                  pl.BlockSpec((None, MLA_O_W, tn), lambda i, j: (l, 0, j)),
                  pl.BlockSpec((tm, tn), lambda i, j: (i, j)),
                  pl.BlockSpec((tm, tn), lambda i, j: (i, ngb + j))],
        out_specs=pl.BlockSpec((tm, tn), lambda i, j: (i, j)),
        compiler_params=_cparams(2),
        name="gated",
    )(o_a, o_b, w_a, w_b, h_g, h_g)


def _layer_norm(z, g, b):
    mu = jnp.mean(z, axis=-1, keepdims=True)
    zc = z - mu
    var = jnp.mean(zc * zc, axis=-1, keepdims=True)
    return zc * lax.rsqrt(var + NORM_EPS) * g + b


def _split_bf16(x):
    hi = x.astype(BF16)
    lo = (x - hi.astype(F32)).astype(BF16)
    return hi, lo


def _route_record(lg):
    lane = lax.broadcasted_iota(I32, lg.shape, 1)
    lane_f = lane.astype(F32)
    big = float(ROUTE_W)
    gmask = lane < N_GROUPS
    g_max = jnp.max(jnp.where(gmask, lg, NEG_INF), axis=1, keepdims=True)
    g_idx = jnp.min(jnp.where(gmask & (lg == g_max), lane_f, big), axis=1, keepdims=True)
    g_w = 1.0 / jnp.sum(jnp.where(gmask, jnp.exp(lg - g_max), 0.0), axis=1, keepdims=True)

    lo = N_GROUPS + EXPERTS_PER_GROUP * g_idx
    sel = (lane_f >= lo) & (lane_f < lo + EXPERTS_PER_GROUP)
    e_max = jnp.max(jnp.where(sel, lg, NEG_INF), axis=1, keepdims=True)
    pe = jnp.where(sel, jnp.exp(lg - e_max), 0.0)
    prob = pe / jnp.sum(pe, axis=1, keepdims=True)
    p1 = jnp.max(jnp.where(sel, prob, -1.0), axis=1, keepdims=True)
    i1 = jnp.min(jnp.where(sel & (prob == p1), lane_f, big), axis=1, keepdims=True)
    sel2 = sel & (lane_f != i1)
    p2 = jnp.max(jnp.where(sel2, prob, -1.0), axis=1, keepdims=True)
    i2 = jnp.min(jnp.where(sel2 & (prob == p2), lane_f, big), axis=1, keepdims=True)
    tot = p1 + p2
    w1 = g_w * (p1 / tot)
    w2 = g_w * (p2 / tot)
    rec = jnp.where(lane == 0, i1 - N_GROUPS,
          jnp.where(lane == 1, i2 - N_GROUPS,
          jnp.where(lane == 2, w1, jnp.where(lane == 3, w2, 0.0))))
    return rec


def _outproj_kernel(y_ref, w_ref, x_ref, g_ref, b_ref, wr_ref, br_ref,
                    xo_ref, xt_ref, route_ref, *, alpha):
    mix = jnp.dot(y_ref[...], w_ref[...].astype(BF16), preferred_element_type=F32)
    xn = _layer_norm(alpha * x_ref[...] + mix, g_ref[...], b_ref[...])
    xo_ref[...] = xn
    _wide_to_tiled(xn, xt_ref)
    xh, xl = _split_bf16(xn)
    wh, wl = _split_bf16(wr_ref[...])
    lg = (jnp.dot(xh, wh, preferred_element_type=F32) + jnp.dot(xl, wh, preferred_element_type=F32)
          + jnp.dot(xh, wl, preferred_element_type=F32) + br_ref[...])
    route_ref[...] = _route_record(lg)


def _outproj_ln_route(y, w_out, x, ln_g, ln_b, w_r, b_r, l, alpha, tm):
    m, d = x.shape
    tm = min(tm, m)
    const = dict(pipeline_mode=pl.Buffered(1))
    return pl.pallas_call(
        functools.partial(_outproj_kernel, alpha=alpha),
        out_shape=(SDS((m, d), F32), SDS((m * ROW_TILES, LANES), U32), SDS((m, ROUTE_W), F32)),
        grid=(m // tm,),
        in_specs=[pl.BlockSpec((tm, d), lambda i: (i, 0)),
                  pl.BlockSpec((None, d, d), lambda i: (l, 0, 0), **const),
                  pl.BlockSpec((tm, d), lambda i: (i, 0)),
                  pl.BlockSpec((None, 1, d), lambda i: (l, 0, 0)),
                  pl.BlockSpec((None, 1, d), lambda i: (l, 0, 0)),
                  pl.BlockSpec((None, d, ROUTE_W), lambda i: (0, 0, 0)),
                  pl.BlockSpec((None, 1, ROUTE_W), lambda i: (0, 0, 0))],
        out_specs=(pl.BlockSpec((tm, d), lambda i: (i, 0)),
                   pl.BlockSpec((tm * ROW_TILES, LANES), lambda i: (i, 0)),
                   pl.BlockSpec((tm, ROUTE_W), lambda i: (i, 0))),
        compiler_params=_cparams(1),
        name="outproj_ln_route",
    )(y, w_out, x, ln_g, ln_b, w_r, b_r)


def _row_copy(src_hbm, row, dst, r, sem):
    src = src_hbm.at[pl.ds(pl.multiple_of(row * ROW_TILES, ROW_TILES), ROW_TILES), :]
    return pltpu.make_async_copy(src, dst.at[pl.ds(pl.multiple_of(r * ROW_TILES, ROW_TILES), ROW_TILES), :], sem)


HI_MASK = 0xFFFF0000


def _tiled_to_wide(ref, n):
    words = [ref[pl.ds(s, n, stride=ROW_TILES), :] for s in range(ROW_TILES)]
    lo = [pltpu.bitcast(w << 16, F32) for w in words]
    hi = [pltpu.bitcast(w & jnp.uint32(HI_MASK), F32) for w in words]
    return jnp.concatenate(lo + hi, axis=1)


def _wide_to_tiled(val, ref):
    n = val.shape[0]
    bits = pltpu.bitcast(val.astype(BF16).astype(F32), U32)
    for s in range(ROW_TILES):
        lo = bits[:, s * LANES:(s + 1) * LANES] >> 16
        hi = bits[:, (ROW_TILES + s) * LANES:(ROW_TILES + s + 1) * LANES]
        ref[pl.ds(s, n, stride=ROW_TILES), :] = hi | lo


def _gather_rows(src_hbm, idx_ref, base, dst, sem, n):
    assert n % GATHER_UNROLL == 0

    def issue(i, c):
        for u in range(GATHER_UNROLL):
            r = i * GATHER_UNROLL + u
            _row_copy(src_hbm, idx_ref[base + r], dst, r, sem).start(priority=u % DMA_QUEUES)
        return c

    lax.fori_loop(0, n // GATHER_UNROLL, issue, 0)


def _gather_wait(src_hbm, dst, sem, n):
    def wait(r, c):
        _row_copy(src_hbm, 0, dst, r, sem).wait()
        return c

    lax.fori_loop(0, n, wait, 0, unroll=GATHER_UNROLL)


def _moe_kernel(bs_ref, nb_ref, nu_ref, tok_ref, x_hbm, wg_ref, wu_ref, wd_ref, y_hbm,
                xbuf, obuf, gsem, osem, *, bm, n_blocks):
    e = pl.program_id(0)
    n_used = nu_ref[0]

    def out_copy(g, slot):
        rows = pl.ds(pl.multiple_of(g * (bm * ROW_TILES), bm * ROW_TILES), bm * ROW_TILES)
        return pltpu.make_async_copy(obuf.at[slot], y_hbm.at[rows, :], osem.at[slot])

    @pl.when(e == 0)
    def _():
        for k in range(GATHER_AHEAD):
            _gather_rows(x_hbm, tok_ref, min(k, n_blocks - 1) * bm, xbuf.at[k], gsem.at[k], bm)

    def body(j, c):
        g = bs_ref[e] + j
        slot = lax.rem(g, 2)
        gslot = lax.rem(g, GATHER_SLOTS)
        nslot = lax.rem(g + GATHER_AHEAD, GATHER_SLOTS)

        _gather_wait(x_hbm, xbuf.at[gslot], gsem.at[gslot], bm)

        @pl.when(g >= 2)
        def _():
            out_copy(g - 2, slot).wait()

        g_next = jnp.minimum(g + GATHER_AHEAD, n_blocks - 1)
        for r in range(bm):
            _row_copy(x_hbm, tok_ref[g_next * bm + r], xbuf.at[nslot], r,
                      gsem.at[nslot]).start(priority=r % DMA_QUEUES)

        xb = _tiled_to_wide(xbuf.at[gslot], bm).astype(BF16)
        gt = _dot_nt(xb, wg_ref[...].astype(BF16))
        up = _dot_nt(xb, wu_ref[...].astype(BF16))
        hid = ((gt * _sigmoid(gt)) * up).astype(BF16)
        _wide_to_tiled(jnp.dot(hid, wd_ref[...].astype(BF16), preferred_element_type=F32),
                       obuf.at[slot])
        out_copy(g, slot).start()
        return c

    lax.fori_loop(0, nb_ref[e], body, 0)

    @pl.when(e == pl.num_programs(0) - 1)
    def _():
        @pl.when(n_used >= 2)
        def _():
            out_copy(n_used - 2, lax.rem(n_used - 2, 2)).wait()

        out_copy(n_used - 1, lax.rem(n_used - 1, 2)).wait()
        for k in range(GATHER_AHEAD):
            dslot = lax.rem(n_used + k, GATHER_SLOTS)
            _gather_wait(x_hbm, xbuf.at[dslot], gsem.at[dslot], bm)

        obuf[0] = jnp.zeros(obuf.shape[1:], obuf.dtype)

        def fill(g, c):
            cp = out_copy(g, 0)
            cp.start()
            cp.wait()
            return c

        lax.fori_loop(n_used, n_blocks, fill, 0)


def _moe_ffn(x_tiled, bstart, nblk, n_used, row_tok, wgt, wut, wd, l, bm):
    f, d = wd.shape[2], wd.shape[3]
    assert d == 2 * ROW_TILES * LANES
    n_blocks = row_tok.shape[0] // bm
    w_spec = pl.BlockSpec((None, None, f, d), lambda e, bs, nb, nu, tok: (l, e, 0, 0))
    grid_spec = pltpu.PrefetchScalarGridSpec(
        num_scalar_prefetch=4,
        grid=(N_EXPERTS,),
        in_specs=[pl.BlockSpec(memory_space=pl.ANY), w_spec, w_spec, w_spec],
        out_specs=pl.BlockSpec(memory_space=pl.ANY),
        scratch_shapes=[pltpu.VMEM((GATHER_SLOTS, bm * ROW_TILES, LANES), U32),
                        pltpu.VMEM((2, bm * ROW_TILES, LANES), U32),
                        pltpu.SemaphoreType.DMA((GATHER_SLOTS,)), pltpu.SemaphoreType.DMA((2,))],
    )
    return pl.pallas_call(
        functools.partial(_moe_kernel, bm=bm, n_blocks=n_blocks),
        out_shape=SDS((n_blocks * bm * ROW_TILES, LANES), U32),
        grid_spec=grid_spec,
        compiler_params=_cparams(1),
        name="moe_ffn",
    )(bstart, nblk, n_used, row_tok, x_tiled, wgt, wut, wd)


def _combine_kernel(pos_ref, yb_hbm, x_ref, route_ref, g_ref, b_ref, xo_ref, xbo_ref,
                    buf, sems, *, tm, t_total, alpha):
    i = pl.program_id(0)
    slot = lax.rem(i, 2)

    def gather(step, s):
        for k in range(TOP_K):
            _gather_rows(yb_hbm, pos_ref, k * t_total + step * tm, buf.at[s, k],
                         sems.at[s * TOP_K + k], tm)

    @pl.when(i == 0)
    def _():
        gather(0, 0)

    @pl.when(i + 1 < pl.num_programs(0))
    def _():
        gather(i + 1, 1 - slot)

    for k in range(TOP_K):
        _gather_wait(yb_hbm, buf.at[slot, k], sems.at[slot * TOP_K + k], tm)

    rec = route_ref[...]
    z = (alpha * x_ref[...] + rec[:, 2:3] * _tiled_to_wide(buf.at[slot, 0], tm)
         + rec[:, 3:4] * _tiled_to_wide(buf.at[slot, 1], tm))
    xn = _layer_norm(z, g_ref[...], b_ref[...])
    xo_ref[...] = xn
    xbo_ref[...] = xn.astype(BF16)


def _combine_ln(pos_k, yb, x, route, ln_g, ln_b, l, alpha, tm):
    t, d = x.shape
    tm = min(tm, t)
    grid_spec = pltpu.PrefetchScalarGridSpec(
        num_scalar_prefetch=1,
        grid=(t // tm,),
        in_specs=[pl.BlockSpec(memory_space=pl.ANY),
                  pl.BlockSpec((tm, d), lambda i, pos: (i, 0)),
                  pl.BlockSpec((tm, ROUTE_W), lambda i, pos: (i, 0)),
                  pl.BlockSpec((None, 1, d), lambda i, pos: (l, 0, 0)),
                  pl.BlockSpec((None, 1, d), lambda i, pos: (l, 0, 0))],
        out_specs=(pl.BlockSpec((tm, d), lambda i, pos: (i, 0)),
                   pl.BlockSpec((tm, d), lambda i, pos: (i, 0))),
        scratch_shapes=[pltpu.VMEM((2, TOP_K, tm * ROW_TILES, LANES), U32),
                        pltpu.SemaphoreType.DMA((2 * TOP_K,))],
    )
    return pl.pallas_call(
        functools.partial(_combine_kernel, tm=tm, t_total=t, alpha=alpha),
        out_shape=(SDS((t, d), F32), SDS((t, d), BF16)),
        grid_spec=grid_spec,
        compiler_params=_cparams(1),
        name="combine_ln",
    )(pos_k, yb, x, route, ln_g, ln_b)


def _route_meta(e_ids, bm):
    t = e_ids.shape[0]
    a = t * TOP_K
    flat_e = e_ids.reshape(a)
    onehot = (flat_e[:, None] == jnp.arange(N_EXPERTS, dtype=I32)[None, :]).astype(I32)
    csum = jnp.cumsum(onehot, axis=0)
    rank = jnp.take_along_axis(csum, flat_e[:, None], axis=1)[:, 0] - 1
    counts = csum[-1]
    nblk = (counts + bm - 1) // bm
    bend = jnp.cumsum(nblk)
    bstart = bend - nblk
    dest = (bstart[flat_e] * bm + rank).astype(I32)
    n_blocks = a // bm + N_EXPERTS
    row_tok = jnp.zeros((n_blocks * bm,), I32).at[dest].set(jnp.arange(a, dtype=I32) // TOP_K)
    pos_k = dest.reshape(t, TOP_K).T.reshape(a)
    return pos_k, row_tok, bstart.astype(I32), nblk.astype(I32), bend[-1:].astype(I32)


def _swap_halves(w, axis):
    half = w.shape[axis] // 2
    lo = lax.slice_in_dim(w, 0, half, axis=axis)
    hi = lax.slice_in_dim(w, half, 2 * half, axis=axis)
    return jnp.concatenate([-hi, lo], axis=axis)


def _prep_layer_weights(wk_t, w_uq_l, w_ukv_l, w_rg_l, b_rg_l, w_re_l, b_re_l):
    d = wk_t.shape[1]
    w_krt = jnp.concatenate([wk_t, _swap_halves(wk_t, 0)], axis=0)
    wq = w_uq_l.reshape(MLA_Q_RANK, MLA_HEADS, MLA_NOPE_DIM + MLA_ROPE_DIM)
    wq_rope = wq[..., MLA_NOPE_DIM:]
    w_uq2 = jnp.concatenate([wq[..., :MLA_NOPE_DIM], wq_rope, _swap_halves(wq_rope, 2)],
                            axis=-1).reshape(MLA_Q_RANK, MLA_HEADS * MLA_QCAT)
    wkv = w_ukv_l.reshape(MLA_KV_RANK, MLA_HEADS, MLA_NOPE_DIM + MLA_V_DIM)
    w_ukv2 = jnp.concatenate([wkv[..., :MLA_NOPE_DIM].reshape(MLA_KV_RANK, -1),
                              wkv[..., MLA_NOPE_DIM:].reshape(MLA_KV_RANK, -1)], axis=1)
    pad = ROUTE_W - N_GROUPS - N_EXPERTS
    w_r = jnp.concatenate([w_rg_l, w_re_l, jnp.zeros((d, pad), F32)], axis=1)[None]
    b_r = jnp.concatenate([b_rg_l, b_re_l, jnp.zeros((pad,), F32)])[None, None]
    return w_krt, w_uq2, w_ukv2, w_r, b_r


def _rope_tables(seq):
    inv = ROPE_THETA ** (-jnp.arange(0, MLA_ROPE_DIM, 2, dtype=F32) / MLA_ROPE_DIM)
    ang = jnp.arange(seq, dtype=F32)[:, None] * inv[None, :]
    return jnp.cos(ang), jnp.sin(ang)


def kernel(x, w_in, da_lambda, da_subln_g, mla_q_norm_g, mla_w_uq, mla_kv_norm_g, mla_w_ukv, w_branch_a, w_branch_b, w_out, rel_bias, ln1_g, ln1_b, router_w_group, router_b_group, router_w_expert, router_b_expert, expert_w_gate, expert_w_up, expert_w_down, ln2_g, ln2_b):
    batch, seq, d = x.shape
    depth = w_in.shape[0]
    t = batch * seq
    alpha = (2 * depth) ** 0.25
    main_w = DA_QK_W * 2 + DA_V_W + MLA_Q_RANK + MLA_KV_RANK
    gate0 = main_w + MLA_ROPE_DIM

    cos, sin = _rope_tables(seq)
    tabk = jnp.concatenate([cos, cos, sin, sin], axis=1)
    q_scale = (MLA_NOPE_DIM + MLA_ROPE_DIM) ** -0.5
    tabq = q_scale * jnp.concatenate([jnp.ones((seq, MLA_NOPE_DIM), F32), tabk], axis=1)

    subln_g = da_subln_g[:, None, :]
    ln1_g3, ln1_b3 = ln1_g[:, None, :], ln1_b[:, None, :]
    ln2_g3, ln2_b3 = ln2_g[:, None, :], ln2_b[:, None, :]

    wt_in = jnp.swapaxes(w_in, 1, 2)
    wgt = jnp.swapaxes(expert_w_gate, 2, 3)
    wut = jnp.swapaxes(expert_w_up, 2, 3)

    xf = x.reshape(t, d)
    xb = xf.astype(BF16)
    for l in range(depth):
        lam_init = 0.8 - 0.6 * math.exp(-0.3 * l)
        w_krt, w_uq2, w_ukv2, w_r, b_r = _prep_layer_weights(
            wt_in[l, main_w:gate0, :], mla_w_uq[l], mla_w_ukv[l], router_w_group[l],
            router_b_group[l], router_w_expert[l], router_b_expert[l])

        h_main = _matmul_nt(xb, wt_in, l, 0, main_w, 4096, 256, BF16, "mm_main")
        h_g = _matmul_nt(xb, wt_in, l, gate0, 2 * d, 2048, 512, BF16, "mm_gates")
        kr2 = _krope(xb, w_krt, tabk, seq, 512)
        o_a = _da_attention(h_main, rel_bias, da_lambda, subln_g, l, lam_init, batch, seq)
        q_cat = _uq(h_main, mla_q_norm_g[l][None], w_uq2, tabq, seq, 512)
        kv = _ukv(h_main, mla_kv_norm_g[l][None], w_ukv2, 512)
        o_b = _mla_attention(q_cat, kv, kr2, batch, seq)
        y = _gated(o_a, o_b, w_branch_a, w_branch_b, h_g, l, 1024, 512)
        x1, x1_tiled, route = _outproj_ln_route(y, w_out, xf, ln1_g3, ln1_b3, w_r, b_r, l, alpha, 256)

        e_ids = route[:, :TOP_K].astype(I32)
        pos_k, row_tok, bstart, nblk, n_used = _route_meta(e_ids, MOE_BLOCK)
        yb = _moe_ffn(x1_tiled, bstart, nblk, n_used, row_tok, wgt, wut, expert_w_down, l, MOE_BLOCK)
        xf, xb = _combine_ln(pos_k, yb, x1, route, ln2_g3, ln2_b3, l, alpha, 256)
    return xf.reshape(batch, seq, d)
```

```python
import functools
import math

import jax
import jax.numpy as jnp
from jax import lax
from jax.experimental import pallas as pl
from jax.experimental.pallas import tpu as pltpu

F32 = jnp.float32
BF16 = jnp.bfloat16
I32 = jnp.int32
U32 = jnp.uint32
SDS = jax.ShapeDtypeStruct

DA_HEADS = 8
DA_HEAD_DIM = 64
DA_V_DIM = 2 * DA_HEAD_DIM
MLA_HEADS = 8
MLA_Q_RANK = 768
MLA_KV_RANK = 512
MLA_NOPE_DIM = 128
MLA_ROPE_DIM = 64
MLA_V_DIM = 128
ROPE_THETA = 10000.0
REL_BUCKETS = 32
REL_MAX_DIST = 128
REL_MAX_EXACT = REL_BUCKETS // 2
N_GROUPS = 4
EXPERTS_PER_GROUP = 8
N_EXPERTS = N_GROUPS * EXPERTS_PER_GROUP
TOP_K = 2
NORM_EPS = 1e-5
NEG_INF = -1e30

DA_QK_W = DA_HEADS * 2 * DA_HEAD_DIM
DA_V_W = DA_HEADS * DA_V_DIM
MLA_O_W = MLA_HEADS * MLA_V_DIM
MLA_QCAT = 2 * MLA_NOPE_DIM

LANES = 128
SUBLANES = 8
VMEM_LIMIT = 56 * 1024 * 1024

ATT_K_BLOCK = 512
DA_Q_BLOCK = 256
MOE_BLOCK = 256
ROUTE_W = LANES
GATHER_UNROLL = 8
ROW_TILES = 8
HEADS_PER_STEP = 8
DMA_QUEUES = 2
GATHER_AHEAD = 3
GATHER_SLOTS = GATHER_AHEAD + 1


def _cparams(n_axes):
    return pltpu.CompilerParams(dimension_semantics=("arbitrary",) * n_axes,
                                vmem_limit_bytes=VMEM_LIMIT)


def _dot_nt(a, b):
    return lax.dot_general(a, b, (((1,), (1,)), ((), ())), preferred_element_type=F32)


def _mm_nt_kernel(a_ref, w_ref, o_ref):
    o_ref[...] = _dot_nt(a_ref[...], w_ref[0].astype(BF16)).astype(o_ref.dtype)


def _matmul_nt(a, wt3, l, row0, n_rows, tm, tn, out_dtype, name):
    m, k = a.shape
    tm = min(tm, m)
    assert m % tm == 0 and n_rows % tn == 0
    if row0 % tn == 0:
        r0 = row0 // tn
        w_spec = pl.BlockSpec((1, tn, k), lambda i, j: (l, r0 + j, 0))
    else:
        w_spec = pl.BlockSpec((pl.Element(1), pl.Element(tn), pl.Element(k)),
                              lambda i, j: (l, pl.multiple_of(row0 + j * tn, SUBLANES), 0))
    return pl.pallas_call(
        _mm_nt_kernel,
        out_shape=SDS((m, n_rows), out_dtype),
        grid=(m // tm, n_rows // tn),
        in_specs=[pl.BlockSpec((tm, k), lambda i, j: (i, 0)), w_spec],
        out_specs=pl.BlockSpec((tm, tn), lambda i, j: (i, j)),
        compiler_params=_cparams(2),
        name=name,
    )(a, wt3)


def _krope_kernel(a_ref, w_ref, tab_ref, o_ref):
    t = _dot_nt(a_ref[...], w_ref[...].astype(BF16))
    t = t * tab_ref[...]
    o_ref[...] = (t + pltpu.roll(t, MLA_ROPE_DIM, axis=1)).astype(o_ref.dtype)


def _krope(xb, w_kr, tabk, seq, tm):
    m, k = xb.shape
    tm = min(tm, seq)
    nsb = seq // tm
    return pl.pallas_call(
        _krope_kernel,
        out_shape=SDS((m, LANES), BF16),
        grid=(m // tm,),
        in_specs=[pl.BlockSpec((tm, k), lambda i: (i, 0)),
                  pl.BlockSpec((LANES, k), lambda i: (0, 0)),
                  pl.BlockSpec((tm, LANES), lambda i: (i % nsb, 0))],
        out_specs=pl.BlockSpec((tm, LANES), lambda i: (i, 0)),
        compiler_params=_cparams(1),
        name="krope",
    )(xb, w_kr, tabk)


def _rms(c, g):
    return c * lax.rsqrt(jnp.mean(c * c, axis=-1, keepdims=True) + NORM_EPS) * g


def _uq_kernel(c_ref, g_ref, w_ref, tab_ref, o_ref):
    n = _rms(c_ref[...].astype(F32), g_ref[...])
    acc = jnp.dot(n.astype(BF16), w_ref[...].astype(BF16), preferred_element_type=F32)
    tab = tab_ref[...]
    for h in range(MLA_HEADS):
        sl = slice(h * MLA_QCAT, (h + 1) * MLA_QCAT)
        o_ref[:, sl] = (acc[:, sl] * tab).astype(o_ref.dtype)


def _uq(h_main, g, w_uq2, tabq, seq, tm):
    m = h_main.shape[0]
    tm = min(tm, seq)
    nsb = seq // tm
    cq_blk = (DA_QK_W * 2 + DA_V_W) // MLA_Q_RANK
    n_out = MLA_HEADS * MLA_QCAT
    return pl.pallas_call(
        _uq_kernel,
        out_shape=SDS((m, n_out), BF16),
        grid=(m // tm,),
        in_specs=[pl.BlockSpec((tm, MLA_Q_RANK), lambda i: (i, cq_blk)),
                  pl.BlockSpec((1, MLA_Q_RANK), lambda i: (0, 0)),
                  pl.BlockSpec((MLA_Q_RANK, n_out), lambda i: (0, 0)),
                  pl.BlockSpec((tm, MLA_QCAT), lambda i: (i % nsb, 0))],
        out_specs=pl.BlockSpec((tm, n_out), lambda i: (i, 0)),
        compiler_params=_cparams(1),
        name="mla_uq",
    )(h_main, g, w_uq2, tabq)


def _ukv_kernel(c0_ref, c1_ref, g_ref, w_ref, o_ref):
    c = jnp.concatenate([c0_ref[...], c1_ref[...]], axis=1).astype(F32)
    n = _rms(c, g_ref[...])
    o_ref[...] = jnp.dot(n.astype(BF16), w_ref[...].astype(BF16),
                         preferred_element_type=F32).astype(o_ref.dtype)


def _ukv(h_main, g, w_ukv2, tm):
    m = h_main.shape[0]
    tm = min(tm, m)
    half = MLA_KV_RANK // 2
    b0 = (DA_QK_W * 2 + DA_V_W + MLA_Q_RANK) // half
    n_out = w_ukv2.shape[1]
    return pl.pallas_call(
        _ukv_kernel,
        out_shape=SDS((m, n_out), BF16),
        grid=(m // tm,),
        in_specs=[pl.BlockSpec((tm, half), lambda i: (i, b0)),
                  pl.BlockSpec((tm, half), lambda i: (i, b0 + 1)),
                  pl.BlockSpec((1, MLA_KV_RANK), lambda i: (0, 0)),
                  pl.BlockSpec((MLA_KV_RANK, n_out), lambda i: (0, 0))],
        out_specs=pl.BlockSpec((tm, n_out), lambda i: (i, 0)),
        compiler_params=_cparams(1),
        name="mla_ukv",
    )(h_main, h_main, g, w_ukv2)


def _flash_init(m_scr, acc_scr):
    m_scr[...] = jnp.full(m_scr.shape, NEG_INF, F32)
    acc_scr[...] = jnp.zeros(acc_scr.shape, F32)


def _flash_update(s, v, m_scr, acc_scr):
    v1 = jnp.concatenate([v, jnp.ones(v.shape, v.dtype)], axis=1)
    m_prev = m_scr[...]
    m_new = jnp.maximum(m_prev, jnp.max(s, axis=1, keepdims=True))
    p = jnp.exp(s - jnp.concatenate([m_new] * (s.shape[1] // LANES), axis=1))
    alpha = jnp.exp(m_prev - m_new)
    acc_scr[...] = (jnp.concatenate([alpha, alpha], axis=1) * acc_scr[...]
                    + jnp.dot(p.astype(BF16), v1, preferred_element_type=F32))
    m_scr[...] = m_new


def _flash_result(acc_scr):
    acc = acc_scr[...]
    return acc[:, :LANES] / acc[:, LANES:]


def _t5_bias_blocks(rb_ref, h, bias_scr, tq, tk):
    row = lax.broadcasted_iota(I32, (tq, tk), 0)
    col = lax.broadcasted_iota(I32, (tq, tk), 1)
    far = rb_ref[REL_BUCKETS - 1, h]
    for d in range(bias_scr.shape[0]):
        n = row - col + d * tq
        nn = jnp.maximum(n, 0)
        nf = jnp.maximum(nn, 1).astype(F32)
        large = REL_MAX_EXACT + (jnp.log(nf / REL_MAX_EXACT) / math.log(REL_MAX_DIST / REL_MAX_EXACT)
                                 * (REL_BUCKETS - REL_MAX_EXACT)).astype(I32)
        large = jnp.minimum(large, REL_BUCKETS - 1)
        bucket = jnp.where(nn < REL_MAX_EXACT, nn, large)
        val = jnp.zeros((tq, tk), F32)
        for bb in range(REL_BUCKETS):
            val = jnp.where(bucket == bb, rb_ref[bb, h], val)
        bias_scr[d] = jnp.where(n >= 0, val - far, NEG_INF)


def _da_kernel(rb_ref, lamv_ref, q_ref, k_ref, v_ref, g_ref, o_ref,
               bias_scr, m_scr, acc_scr, *, tq, tk, lam_init):
    hp = pl.program_id(0)
    b = pl.program_id(1)
    qi = pl.program_id(2)
    ratio = tk // tq
    jd = qi // ratio
    par = qi % ratio
    hd = 2 * DA_HEAD_DIM
    heads = range(HEADS_PER_STEP)

    @pl.when((b == 0) & (qi == 0))
    def _():
        for hh in heads:
            _t5_bias_blocks(rb_ref, hp * HEADS_PER_STEP + hh, bias_scr.at[hh], tq, tk)

    scale = DA_HEAD_DIM ** -0.5
    q_all = q_ref[...]
    q2 = []
    for hh in heads:
        q = q_all[:, hh * hd:(hh + 1) * hd]
        lane = lax.broadcasted_iota(I32, q.shape, 1)
        zero = jnp.zeros_like(q)
        q2.append(jnp.concatenate([jnp.where(lane < DA_HEAD_DIM, q, zero),
                                   jnp.where(lane >= DA_HEAD_DIM, q, zero)], axis=0) * scale)
        _flash_init(m_scr.at[hh], acc_scr.at[hh])

    def block(j, bias_idx):
        off = pl.multiple_of(j * tk, tk)
        kb = k_ref[pl.ds(off, tk), :]
        vb = v_ref[pl.ds(off, tk), :]
        for hh in heads:
            s = _dot_nt(q2[hh], kb[:, hh * hd:(hh + 1) * hd])
            if bias_idx is not None:
                bias = bias_scr[hh, bias_idx]
                s = s + jnp.concatenate([bias, bias], axis=0)
            _flash_update(s, vb[:, hh * DA_V_DIM:(hh + 1) * DA_V_DIM], m_scr.at[hh], acc_scr.at[hh])

    def far_body(j, c):
        block(j, None)
        return c

    lax.fori_loop(0, jd - 1, far_body, 0)

    @pl.when((jd >= 1) & (par == 0))
    def _():
        block(jd - 1, ratio)

    @pl.when((jd >= 1) & (par != 0))
    def _():
        block(jd - 1, None)

    block(jd, par)

    lamv = lamv_ref[...]
    lam = (jnp.exp(jnp.sum(lamv[0:1] * lamv[1:2], axis=1, keepdims=True))
           - jnp.exp(jnp.sum(lamv[2:3] * lamv[3:4], axis=1, keepdims=True)) + lam_init)
    for hh in heads:
        o12 = _flash_result(acc_scr.at[hh])
        o = o12[:tq] - lam * o12[tq:]
        o = o * lax.rsqrt(jnp.mean(o * o, axis=1, keepdims=True) + NORM_EPS) * g_ref[...] * (1.0 - lam_init)
        o_ref[:, hh * DA_V_DIM:(hh + 1) * DA_V_DIM] = o.astype(o_ref.dtype)


def _da_attention(h_main, rel_bias, da_lambda, subln_g, l, lam_init, batch, seq):
    tq, tk = DA_Q_BLOCK, ATT_K_BLOCK
    ratio = tk // tq
    assert seq % tk == 0 and tk % tq == 0 and tq + 1 >= REL_MAX_DIST
    nq = seq // tq
    hw = HEADS_PER_STEP * 2 * DA_HEAD_DIM
    vw = HEADS_PER_STEP * DA_V_DIM
    kcol = DA_QK_W // hw
    vcol = 2 * DA_QK_W // vw
    return pl.pallas_call(
        functools.partial(_da_kernel, tq=tq, tk=tk, lam_init=lam_init),
        out_shape=SDS((batch * seq, DA_V_W), BF16),
        grid=(DA_HEADS // HEADS_PER_STEP, batch, nq),
        in_specs=[pl.BlockSpec(memory_space=pltpu.SMEM),
                  pl.BlockSpec((None, 4, DA_HEAD_DIM), lambda h, b, qi: (l, 0, 0)),
                  pl.BlockSpec((tq, hw), lambda h, b, qi: (b * nq + qi, h)),
                  pl.BlockSpec((seq, hw), lambda h, b, qi: (b, kcol + h)),
                  pl.BlockSpec((seq, vw), lambda h, b, qi: (b, vcol + h)),
                  pl.BlockSpec((None, 1, DA_V_DIM), lambda h, b, qi: (l, 0, 0))],
        out_specs=pl.BlockSpec((tq, vw), lambda h, b, qi: (b * nq + qi, h)),
        scratch_shapes=[pltpu.VMEM((HEADS_PER_STEP, ratio + 1, tq, tk), F32),
                        pltpu.VMEM((HEADS_PER_STEP, 2 * tq, LANES), F32),
                        pltpu.VMEM((HEADS_PER_STEP, 2 * tq, 2 * LANES), F32)],
        compiler_params=_cparams(3),
        name="da_attn",
    )(rel_bias, da_lambda, h_main, h_main, h_main, subln_g)


def _mla_kernel(q_ref, kn_ref, kr_ref, v_ref, o_ref, m_scr, acc_scr, *, blk):
    qi = pl.program_id(2)
    heads = range(HEADS_PER_STEP)
    q_all = q_ref[...]
    q = [q_all[:, hh * MLA_QCAT:(hh + 1) * MLA_QCAT] for hh in heads]
    for hh in heads:
        _flash_init(m_scr.at[hh], acc_scr.at[hh])

    def block(j, masked):
        off = pl.multiple_of(j * blk, blk)
        kn = kn_ref[pl.ds(off, blk), :]
        kr = kr_ref[pl.ds(off, blk), :]
        vb = v_ref[pl.ds(off, blk), :]
        for hh in heads:
            kcat = jnp.concatenate([kn[:, hh * MLA_NOPE_DIM:(hh + 1) * MLA_NOPE_DIM], kr], axis=1)
            s = _dot_nt(q[hh], kcat)
            if masked:
                row = lax.broadcasted_iota(I32, s.shape, 0)
                col = lax.broadcasted_iota(I32, s.shape, 1)
                s = jnp.where(col <= row, s, NEG_INF)
            _flash_update(s, vb[:, hh * MLA_V_DIM:(hh + 1) * MLA_V_DIM], m_scr.at[hh], acc_scr.at[hh])

    def far_body(j, c):
        block(j, False)
        return c

    lax.fori_loop(0, qi, far_body, 0)
    block(qi, True)
    for hh in heads:
        o_ref[:, hh * MLA_V_DIM:(hh + 1) * MLA_V_DIM] = _flash_result(acc_scr.at[hh]).astype(o_ref.dtype)


def _mla_attention(q_cat, kv, kr2, batch, seq):
    blk = min(ATT_K_BLOCK, seq)
    nq = seq // blk
    hps = HEADS_PER_STEP
    vcol = MLA_HEADS // hps
    return pl.pallas_call(
        functools.partial(_mla_kernel, blk=blk),
        out_shape=SDS((batch * seq, MLA_O_W), BF16),
        grid=(batch, MLA_HEADS // hps, nq),
        in_specs=[pl.BlockSpec((blk, hps * MLA_QCAT), lambda b, h, qi: (b * nq + qi, h)),
                  pl.BlockSpec((seq, hps * MLA_NOPE_DIM), lambda b, h, qi: (b, h)),
                  pl.BlockSpec((seq, LANES), lambda b, h, qi: (b, 0)),
                  pl.BlockSpec((seq, hps * MLA_V_DIM), lambda b, h, qi: (b, vcol + h))],
        out_specs=pl.BlockSpec((blk, hps * MLA_V_DIM), lambda b, h, qi: (b * nq + qi, h)),
        scratch_shapes=[pltpu.VMEM((hps, blk, LANES), F32),
                        pltpu.VMEM((hps, blk, 2 * LANES), F32)],
        compiler_params=_cparams(3),
        name="mla_attn",
    )(q_cat, kv, kr2, kv)


def _sigmoid(x):
    return 1.0 / (1.0 + jnp.exp(-x))


def _gated_kernel(oa_ref, ob_ref, wa_ref, wb_ref, ga_ref, gb_ref, o_ref):
    ya = jnp.dot(oa_ref[...], wa_ref[...].astype(BF16), preferred_element_type=F32)
    yb = jnp.dot(ob_ref[...], wb_ref[...].astype(BF16), preferred_element_type=F32)
    y = _sigmoid(ga_ref[...].astype(F32)) * ya + _sigmoid(gb_ref[...].astype(F32)) * yb
    o_ref[...] = y.astype(o_ref.dtype)


def _gated(o_a, o_b, w_a, w_b, h_g, l, tm, tn):
    m = o_a.shape[0]
    d = w_a.shape[2]
    tm = min(tm, m)
    ngb = d // tn
    return pl.pallas_call(
        _gated_kernel,
        out_shape=SDS((m, d), BF16),
        grid=(m // tm, ngb),
        in_specs=[pl.BlockSpec((tm, DA_V_W), lambda i, j: (i, 0)),
                  pl.BlockSpec((tm, MLA_O_W), lambda i, j: (i, 0)),
                  pl.BlockSpec((None, DA_V_W, tn), lambda i, j: (l, 0, j)),
                  pl.BlockSpec((None, MLA_O_W, tn), lambda i, j: (l, 0, j)),
                  pl.BlockSpec((tm, tn), lambda i, j: (i, j)),
                  pl.BlockSpec((tm, tn), lambda i, j: (i, ngb + j))],
        out_specs=pl.BlockSpec((tm, tn), lambda i, j: (i, j)),
        compiler_params=_cparams(2),
        name="gated",
    )(o_a, o_b, w_a, w_b, h_g, h_g)


def _layer_norm(z, g, b):
    mu = jnp.mean(z, axis=-1, keepdims=True)
    zc = z - mu
    var = jnp.mean(zc * zc, axis=-1, keepdims=True)
    return zc * lax.rsqrt(var + NORM_EPS) * g + b


def _split_bf16(x):
    hi = x.astype(BF16)
    lo = (x - hi.astype(F32)).astype(BF16)
    return hi, lo


def _route_record(lg):
    lane = lax.broadcasted_iota(I32, lg.shape, 1)
    lane_f = lane.astype(F32)
    big = float(ROUTE_W)
    gmask = lane < N_GROUPS
    g_max = jnp.max(jnp.where(gmask, lg, NEG_INF), axis=1, keepdims=True)
    g_idx = jnp.min(jnp.where(gmask & (lg == g_max), lane_f, big), axis=1, keepdims=True)
    g_w = 1.0 / jnp.sum(jnp.where(gmask, jnp.exp(lg - g_max), 0.0), axis=1, keepdims=True)

    lo = N_GROUPS + EXPERTS_PER_GROUP * g_idx
    sel = (lane_f >= lo) & (lane_f < lo + EXPERTS_PER_GROUP)
    e_max = jnp.max(jnp.where(sel, lg, NEG_INF), axis=1, keepdims=True)
    pe = jnp.where(sel, jnp.exp(lg - e_max), 0.0)
    prob = pe / jnp.sum(pe, axis=1, keepdims=True)
    p1 = jnp.max(jnp.where(sel, prob, -1.0), axis=1, keepdims=True)
    i1 = jnp.min(jnp.where(sel & (prob == p1), lane_f, big), axis=1, keepdims=True)
    sel2 = sel & (lane_f != i1)
    p2 = jnp.max(jnp.where(sel2, prob, -1.0), axis=1, keepdims=True)
    i2 = jnp.min(jnp.where(sel2 & (prob == p2), lane_f, big), axis=1, keepdims=True)
    tot = p1 + p2
    w1 = g_w * (p1 / tot)
    w2 = g_w * (p2 / tot)
    rec = jnp.where(lane == 0, i1 - N_GROUPS,
          jnp.where(lane == 1, i2 - N_GROUPS,
          jnp.where(lane == 2, w1, jnp.where(lane == 3, w2, 0.0))))
    return rec


def _outproj_kernel(y_ref, w_ref, x_ref, g_ref, b_ref, wr_ref, br_ref,
                    xo_ref, xt_ref, route_ref, *, alpha):
    mix = jnp.dot(y_ref[...], w_ref[...].astype(BF16), preferred_element_type=F32)
    xn = _layer_norm(alpha * x_ref[...] + mix, g_ref[...], b_ref[...])
    xo_ref[...] = xn
    _wide_to_tiled(xn, xt_ref)
    xh, xl = _split_bf16(xn)
    wh, wl = _split_bf16(wr_ref[...])
    lg = (jnp.dot(xh, wh, preferred_element_type=F32) + jnp.dot(xl, wh, preferred_element_type=F32)
          + jnp.dot(xh, wl, preferred_element_type=F32) + br_ref[...])
    route_ref[...] = _route_record(lg)


def _outproj_ln_route(y, w_out, x, ln_g, ln_b, w_r, b_r, l, alpha, tm):
    m, d = x.shape
    tm = min(tm, m)
    const = dict(pipeline_mode=pl.Buffered(1))
    return pl.pallas_call(
        functools.partial(_outproj_kernel, alpha=alpha),
        out_shape=(SDS((m, d), F32), SDS((m * ROW_TILES, LANES), U32), SDS((m, ROUTE_W), F32)),
        grid=(m // tm,),
        in_specs=[pl.BlockSpec((tm, d), lambda i: (i, 0)),
                  pl.BlockSpec((None, d, d), lambda i: (l, 0, 0), **const),
                  pl.BlockSpec((tm, d), lambda i: (i, 0)),
                  pl.BlockSpec((None, 1, d), lambda i: (l, 0, 0)),
                  pl.BlockSpec((None, 1, d), lambda i: (l, 0, 0)),
                  pl.BlockSpec((None, d, ROUTE_W), lambda i: (0, 0, 0)),
                  pl.BlockSpec((None, 1, ROUTE_W), lambda i: (0, 0, 0))],
        out_specs=(pl.BlockSpec((tm, d), lambda i: (i, 0)),
                   pl.BlockSpec((tm * ROW_TILES, LANES), lambda i: (i, 0)),
                   pl.BlockSpec((tm, ROUTE_W), lambda i: (i, 0))),
        compiler_params=_cparams(1),
        name="outproj_ln_route",
    )(y, w_out, x, ln_g, ln_b, w_r, b_r)


def _row_copy(src_hbm, row, dst, r, sem):
    src = src_hbm.at[pl.ds(pl.multiple_of(row * ROW_TILES, ROW_TILES), ROW_TILES), :]
    return pltpu.make_async_copy(src, dst.at[pl.ds(pl.multiple_of(r * ROW_TILES, ROW_TILES), ROW_TILES), :], sem)


HI_MASK = 0xFFFF0000


def _tiled_to_wide(ref, n):
    words = [ref[pl.ds(s, n, stride=ROW_TILES), :] for s in range(ROW_TILES)]
    lo = [pltpu.bitcast(w << 16, F32) for w in words]
    hi = [pltpu.bitcast(w & jnp.uint32(HI_MASK), F32) for w in words]
    return jnp.concatenate(lo + hi, axis=1)


def _wide_to_tiled(val, ref):
    n = val.shape[0]
    bits = pltpu.bitcast(val.astype(BF16).astype(F32), U32)
    for s in range(ROW_TILES):
        lo = bits[:, s * LANES:(s + 1) * LANES] >> 16
        hi = bits[:, (ROW_TILES + s) * LANES:(ROW_TILES + s + 1) * LANES]
        ref[pl.ds(s, n, stride=ROW_TILES), :] = hi | lo


def _gather_rows(src_hbm, idx_ref, base, dst, sem, n):
    assert n % GATHER_UNROLL == 0

    def issue(i, c):
        for u in range(GATHER_UNROLL):
            r = i * GATHER_UNROLL + u
            _row_copy(src_hbm, idx_ref[base + r], dst, r, sem).start(priority=u % DMA_QUEUES)
        return c

    lax.fori_loop(0, n // GATHER_UNROLL, issue, 0)


def _gather_wait(src_hbm, dst, sem, n):
    def wait(r, c):
        _row_copy(src_hbm, 0, dst, r, sem).wait()
        return c

    lax.fori_loop(0, n, wait, 0, unroll=GATHER_UNROLL)


def _moe_kernel(bs_ref, nb_ref, nu_ref, tok_ref, x_hbm, wg_ref, wu_ref, wd_ref, y_hbm,
                xbuf, obuf, gsem, osem, *, bm, n_blocks):
    e = pl.program_id(0)
    n_used = nu_ref[0]

    def out_copy(g, slot):
        rows = pl.ds(pl.multiple_of(g * (bm * ROW_TILES), bm * ROW_TILES), bm * ROW_TILES)
        return pltpu.make_async_copy(obuf.at[slot], y_hbm.at[rows, :], osem.at[slot])

    @pl.when(e == 0)
    def _():
        for k in range(GATHER_AHEAD):
            _gather_rows(x_hbm, tok_ref, min(k, n_blocks - 1) * bm, xbuf.at[k], gsem.at[k], bm)

    def body(j, c):
        g = bs_ref[e] + j
        slot = lax.rem(g, 2)
        gslot = lax.rem(g, GATHER_SLOTS)
        nslot = lax.rem(g + GATHER_AHEAD, GATHER_SLOTS)

        _gather_wait(x_hbm, xbuf.at[gslot], gsem.at[gslot], bm)

        @pl.when(g >= 2)
        def _():
            out_copy(g - 2, slot).wait()

        g_next = jnp.minimum(g + GATHER_AHEAD, n_blocks - 1)
        for r in range(bm):
            _row_copy(x_hbm, tok_ref[g_next * bm + r], xbuf.at[nslot], r,
                      gsem.at[nslot]).start(priority=r % DMA_QUEUES)

        xb = _tiled_to_wide(xbuf.at[gslot], bm).astype(BF16)
        gt = _dot_nt(xb, wg_ref[...].astype(BF16))
        up = _dot_nt(xb, wu_ref[...].astype(BF16))
        hid = ((gt * _sigmoid(gt)) * up).astype(BF16)
        _wide_to_tiled(jnp.dot(hid, wd_ref[...].astype(BF16), preferred_element_type=F32),
                       obuf.at[slot])
        out_copy(g, slot).start()
        return c

    lax.fori_loop(0, nb_ref[e], body, 0)

    @pl.when(e == pl.num_programs(0) - 1)
    def _():
        @pl.when(n_used >= 2)
        def _():
            out_copy(n_used - 2, lax.rem(n_used - 2, 2)).wait()

        out_copy(n_used - 1, lax.rem(n_used - 1, 2)).wait()
        for k in range(GATHER_AHEAD):
            dslot = lax.rem(n_used + k, GATHER_SLOTS)
            _gather_wait(x_hbm, xbuf.at[dslot], gsem.at[dslot], bm)

        obuf[0] = jnp.zeros(obuf.shape[1:], obuf.dtype)

        def fill(g, c):
            cp = out_copy(g, 0)
            cp.start()
            cp.wait()
            return c

        lax.fori_loop(n_used, n_blocks, fill, 0)


def _moe_ffn(x_tiled, bstart, nblk, n_used, row_tok, wgt, wut, wd, l, bm):
    f, d = wd.shape[2], wd.shape[3]
    assert d == 2 * ROW_TILES * LANES
    n_blocks = row_tok.shape[0] // bm
    w_spec = pl.BlockSpec((None, None, f, d), lambda e, bs, nb, nu, tok: (l, e, 0, 0))
    grid_spec = pltpu.PrefetchScalarGridSpec(
        num_scalar_prefetch=4,
        grid=(N_EXPERTS,),
        in_specs=[pl.BlockSpec(memory_space=pl.ANY), w_spec, w_spec, w_spec],
        out_specs=pl.BlockSpec(memory_space=pl.ANY),
        scratch_shapes=[pltpu.VMEM((GATHER_SLOTS, bm * ROW_TILES, LANES), U32),
                        pltpu.VMEM((2, bm * ROW_TILES, LANES), U32),
                        pltpu.SemaphoreType.DMA((GATHER_SLOTS,)), pltpu.SemaphoreType.DMA((2,))],
    )
    return pl.pallas_call(
        functools.partial(_moe_kernel, bm=bm, n_blocks=n_blocks),
        out_shape=SDS((n_blocks * bm * ROW_TILES, LANES), U32),
        grid_spec=grid_spec,
        compiler_params=_cparams(1),
        name="moe_ffn",
    )(bstart, nblk, n_used, row_tok, x_tiled, wgt, wut, wd)


def _combine_kernel(pos_ref, yb_hbm, x_ref, route_ref, g_ref, b_ref, xo_ref, xbo_ref,
                    buf, sems, *, tm, t_total, alpha):
    i = pl.program_id(0)
    slot = lax.rem(i, 2)

    def gather(step, s):
        for k in range(TOP_K):
            _gather_rows(yb_hbm, pos_ref, k * t_total + step * tm, buf.at[s, k],
                         sems.at[s * TOP_K + k], tm)

    @pl.when(i == 0)
    def _():
        gather(0, 0)

    @pl.when(i + 1 < pl.num_programs(0))
    def _():
        gather(i + 1, 1 - slot)

    for k in range(TOP_K):
        _gather_wait(yb_hbm, buf.at[slot, k], sems.at[slot * TOP_K + k], tm)

    rec = route_ref[...]
    z = (alpha * x_ref[...] + rec[:, 2:3] * _tiled_to_wide(buf.at[slot, 0], tm)
         + rec[:, 3:4] * _tiled_to_wide(buf.at[slot, 1], tm))
    xn = _layer_norm(z, g_ref[...], b_ref[...])
    xo_ref[...] = xn
    xbo_ref[...] = xn.astype(BF16)


def _combine_ln(pos_k, yb, x, route, ln_g, ln_b, l, alpha, tm):
    t, d = x.shape
    tm = min(tm, t)
    grid_spec = pltpu.PrefetchScalarGridSpec(
        num_scalar_prefetch=1,
        grid=(t // tm,),
        in_specs=[pl.BlockSpec(memory_space=pl.ANY),
                  pl.BlockSpec((tm, d), lambda i, pos: (i, 0)),
                  pl.BlockSpec((tm, ROUTE_W), lambda i, pos: (i, 0)),
                  pl.BlockSpec((None, 1, d), lambda i, pos: (l, 0, 0)),
                  pl.BlockSpec((None, 1, d), lambda i, pos: (l, 0, 0))],
        out_specs=(pl.BlockSpec((tm, d), lambda i, pos: (i, 0)),
                   pl.BlockSpec((tm, d), lambda i, pos: (i, 0))),
        scratch_shapes=[pltpu.VMEM((2, TOP_K, tm * ROW_TILES, LANES), U32),
                        pltpu.SemaphoreType.DMA((2 * TOP_K,))],
    )
    return pl.pallas_call(
        functools.partial(_combine_kernel, tm=tm, t_total=t, alpha=alpha),
        out_shape=(SDS((t, d), F32), SDS((t, d), BF16)),
        grid_spec=grid_spec,
        compiler_params=_cparams(1),
        name="combine_ln",
    )(pos_k, yb, x, route, ln_g, ln_b)


def _route_meta(e_ids, bm):
    t = e_ids.shape[0]
    a = t * TOP_K
    flat_e = e_ids.reshape(a)
    onehot = (flat_e[:, None] == jnp.arange(N_EXPERTS, dtype=I32)[None, :]).astype(I32)
    csum = jnp.cumsum(onehot, axis=0)
    rank = jnp.take_along_axis(csum, flat_e[:, None], axis=1)[:, 0] - 1
    counts = csum[-1]
    nblk = (counts + bm - 1) // bm
    bend = jnp.cumsum(nblk)
    bstart = bend - nblk
    dest = (bstart[flat_e] * bm + rank).astype(I32)
    n_blocks = a // bm + N_EXPERTS
    row_tok = jnp.zeros((n_blocks * bm,), I32).at[dest].set(jnp.arange(a, dtype=I32) // TOP_K)
    pos_k = dest.reshape(t, TOP_K).T.reshape(a)
    return pos_k, row_tok, bstart.astype(I32), nblk.astype(I32), bend[-1:].astype(I32)


def _swap_halves(w, axis):
    half = w.shape[axis] // 2
    lo = lax.slice_in_dim(w, 0, half, axis=axis)
    hi = lax.slice_in_dim(w, half, 2 * half, axis=axis)
    return jnp.concatenate([-hi, lo], axis=axis)


def _prep_layer_weights(wk_t, w_uq_l, w_ukv_l, w_rg_l, b_rg_l, w_re_l, b_re_l):
    d = wk_t.shape[1]
    w_krt = jnp.concatenate([wk_t, _swap_halves(wk_t, 0)], axis=0)
    wq = w_uq_l.reshape(MLA_Q_RANK, MLA_HEADS, MLA_NOPE_DIM + MLA_ROPE_DIM)
    wq_rope = wq[..., MLA_NOPE_DIM:]
    w_uq2 = jnp.concatenate([wq[..., :MLA_NOPE_DIM], wq_rope, _swap_halves(wq_rope, 2)],
                            axis=-1).reshape(MLA_Q_RANK, MLA_HEADS * MLA_QCAT)
    wkv = w_ukv_l.reshape(MLA_KV_RANK, MLA_HEADS, MLA_NOPE_DIM + MLA_V_DIM)
    w_ukv2 = jnp.concatenate([wkv[..., :MLA_NOPE_DIM].reshape(MLA_KV_RANK, -1),
                              wkv[..., MLA_NOPE_DIM:].reshape(MLA_KV_RANK, -1)], axis=1)
    pad = ROUTE_W - N_GROUPS - N_EXPERTS
    w_r = jnp.concatenate([w_rg_l, w_re_l, jnp.zeros((d, pad), F32)], axis=1)[None]
    b_r = jnp.concatenate([b_rg_l, b_re_l, jnp.zeros((pad,), F32)])[None, None]
    return w_krt, w_uq2, w_ukv2, w_r, b_r


def _rope_tables(seq):
    inv = ROPE_THETA ** (-jnp.arange(0, MLA_ROPE_DIM, 2, dtype=F32) / MLA_ROPE_DIM)
    ang = jnp.arange(seq, dtype=F32)[:, None] * inv[None, :]
    return jnp.cos(ang), jnp.sin(ang)


def kernel(x, w_in, da_lambda, da_subln_g, mla_q_norm_g, mla_w_uq, mla_kv_norm_g, mla_w_ukv, w_branch_a, w_branch_b, w_out, rel_bias, ln1_g, ln1_b, router_w_group, router_b_group, router_w_expert, router_b_expert, expert_w_gate, expert_w_up, expert_w_down, ln2_g, ln2_b):
    batch, seq, d = x.shape
    depth = w_in.shape[0]
    t = batch * seq
    alpha = (2 * depth) ** 0.25
    main_w = DA_QK_W * 2 + DA_V_W + MLA_Q_RANK + MLA_KV_RANK
    gate0 = main_w + MLA_ROPE_DIM

    cos, sin = _rope_tables(seq)
    tabk = jnp.concatenate([cos, cos, sin, sin], axis=1)
    q_scale = (MLA_NOPE_DIM + MLA_ROPE_DIM) ** -0.5
    tabq = q_scale * jnp.concatenate([jnp.ones((seq, MLA_NOPE_DIM), F32), tabk], axis=1)

    subln_g = da_subln_g[:, None, :]
    ln1_g3, ln1_b3 = ln1_g[:, None, :], ln1_b[:, None, :]
    ln2_g3, ln2_b3 = ln2_g[:, None, :], ln2_b[:, None, :]

    wt_in = jnp.swapaxes(w_in, 1, 2)
    wgt = jnp.swapaxes(expert_w_gate, 2, 3)
    wut = jnp.swapaxes(expert_w_up, 2, 3)

    xf = x.reshape(t, d)
    xb = xf.astype(BF16)
    for l in range(depth):
        lam_init = 0.8 - 0.6 * math.exp(-0.3 * l)
        w_krt, w_uq2, w_ukv2, w_r, b_r = _prep_layer_weights(
            wt_in[l, main_w:gate0, :], mla_w_uq[l], mla_w_ukv[l], router_w_group[l],
            router_b_group[l], router_w_expert[l], router_b_expert[l])

        h_main = _matmul_nt(xb, wt_in, l, 0, main_w, 4096, 256, BF16, "mm_main")
        h_g = _matmul_nt(xb, wt_in, l, gate0, 2 * d, 2048, 512, BF16, "mm_gates")
        kr2 = _krope(xb, w_krt, tabk, seq, 512)
        o_a = _da_attention(h_main, rel_bias, da_lambda, subln_g, l, lam_init, batch, seq)
        q_cat = _uq(h_main, mla_q_norm_g[l][None], w_uq2, tabq, seq, 512)
        kv = _ukv(h_main, mla_kv_norm_g[l][None], w_ukv2, 512)
        o_b = _mla_attention(q_cat, kv, kr2, batch, seq)
        y = _gated(o_a, o_b, w_branch_a, w_branch_b, h_g, l, 1024, 512)
        x1, x1_tiled, route = _outproj_ln_route(y, w_out, xf, ln1_g3, ln1_b3, w_r, b_r, l, alpha, 256)

        e_ids = route[:, :TOP_K].astype(I32)
        pos_k, row_tok, bstart, nblk, n_used = _route_meta(e_ids, MOE_BLOCK)
        yb = _moe_ffn(x1_tiled, bstart, nblk, n_used, row_tok, wgt, wut, expert_w_down, l, MOE_BLOCK)
        xf, xb = _combine_ln(pos_k, yb, x1, route, ln2_g3, ln2_b3, l, alpha, 256)
    return xf.reshape(batch, seq, d)
```
